```python
import math
import jax, jax.numpy as jnp
from jax import lax
import numpy as np

D_MODEL = 1024
BATCH = 32
SEQ = 2048
DEPTH = 1

SSD_D_INNER = D_MODEL
SSD_HEAD_DIM = 64
SSD_N_HEADS = SSD_D_INNER // SSD_HEAD_DIM
SSD_N_GROUPS = 2
SSD_D_STATE = 128
SSD_CONV_WIDTH = 4
SSD_CHUNK = 128
SSD_CONV_DIM = SSD_D_INNER + 2 * SSD_N_GROUPS * SSD_D_STATE
ATTN_HEAD_DIM = 64
ATTN_N_HEADS = D_MODEL // (2 * ATTN_HEAD_DIM)
ATTN_WIDTH = ATTN_N_HEADS * 2 * ATTN_HEAD_DIM
ROPE_THETA = 10000.0
Q_BLOCK = 128
D_FF = 2816
FFN_CONV_WIDTH = 3
NORM_EPS = 1e-6
SUBLN_EPS = 1e-5
IN_SIZES = (SSD_D_INNER, SSD_CONV_DIM, SSD_N_HEADS, ATTN_WIDTH, ATTN_WIDTH, ATTN_WIDTH, D_MODEL, D_MODEL)
IN_COLS = SSD_D_INNER + SSD_CONV_DIM + SSD_N_HEADS + 3 * ATTN_WIDTH + 2 * D_MODEL

kernel_name = "hybrid_ssd_diffattn_convffn"


def _split_points(sizes):
    pts, acc = [], 0
    for s in sizes[:-1]:
        acc += s
        pts.append(acc)
    return pts


def rms_norm(x, w, eps=NORM_EPS):
    xf = x.astype(jnp.float32)
    y = xf * lax.rsqrt(jnp.mean(xf * xf, axis=-1, keepdims=True) + eps)
    return (y * w.astype(jnp.float32)).astype(x.dtype)


def causal_dwconv(x, w, b):
    k, c = w.shape
    y = lax.conv_general_dilated(
        x, w[:, None, :].astype(x.dtype), window_strides=(1,), padding=[(k - 1, 0)],
        dimension_numbers=("NWC", "WIO", "NWC"), feature_group_count=c)
    return y + b.astype(x.dtype)


def rope_tables(seq):
    inv = 1.0 / (ROPE_THETA ** (jnp.arange(0, ATTN_HEAD_DIM, 2, dtype=jnp.float32) / ATTN_HEAD_DIM))
    ang = jnp.arange(seq, dtype=jnp.float32)[:, None] * inv[None, :]
    return jnp.cos(ang), jnp.sin(ang)


def apply_rope(x, cos, sin):
    half = ATTN_HEAD_DIM // 2
    xf = x.astype(jnp.float32)
    x1, x2 = xf[..., :half], xf[..., half:]
    c = cos[None, :, None, None, :]
    s = sin[None, :, None, None, :]
    return jnp.concatenate([x1 * c - x2 * s, x2 * c + x1 * s], axis=-1).astype(x.dtype)


def ssd_chunked(xdt, dA, Bm, Cm):
    b, s, h, p = xdt.shape
    g, n = SSD_N_GROUPS, SSD_D_STATE
    e = h // g
    L = SSD_CHUNK
    c = s // L
    X = xdt.reshape(b, c, L, g, e, p)
    Bc = Bm.reshape(b, c, L, g, n)
    Cc = Cm.reshape(b, c, L, g, n)
    A_cs = jnp.cumsum(dA.reshape(b, c, L, g, e), axis=2).transpose(0, 1, 3, 4, 2)
    causal = jnp.arange(L)[:, None] >= jnp.arange(L)[None, :]
    seg = A_cs[..., :, None] - A_cs[..., None, :]
    Lmat = jnp.exp(jnp.where(causal, seg, -jnp.inf))
    CB = jnp.einsum("bclgn,bcsgn->bcgls", Cc, Bc)
    y_diag = jnp.einsum("bcgls,bcgels,bcsgep->bclgep", CB, Lmat, X)
    decay_states = jnp.exp(A_cs[..., -1:] - A_cs)
    states = jnp.einsum("bclgn,bcgel,bclgep->bcgepn", Bc, decay_states, X)
    chunk_tot = jnp.pad(A_cs[..., -1].transpose(0, 2, 3, 1), ((0, 0), (0, 0), (0, 0), (1, 0)))
    ccs = jnp.cumsum(chunk_tot, axis=-1)
    cmask = jnp.arange(c + 1)[:, None] >= jnp.arange(c + 1)[None, :]
    decay_chunk = jnp.exp(jnp.where(cmask, ccs[..., :, None] - ccs[..., None, :], -jnp.inf))
    states_pad = jnp.concatenate([jnp.zeros_like(states[:, :1]), states], axis=1)
    new_states = jnp.einsum("bgezw,bwgepn->bzgepn", decay_chunk, states_pad)
    prev = new_states[:, :-1]
    y_off = jnp.einsum("bclgn,bcgepn,bcgel->bclgep", Cc, prev, jnp.exp(A_cs))
    return (y_diag + y_off).reshape(b, s, h, p)


def ssd_branch(z, xbc, dt_raw, conv_w, conv_b, dt_bias, a_log, d_skip, norm_w):
    b, s, _ = z.shape
    xbc = jax.nn.silu(causal_dwconv(xbc, conv_w, conv_b))
    gn = SSD_N_GROUPS * SSD_D_STATE
    xs, Bm, Cm = jnp.split(xbc, [SSD_D_INNER, SSD_D_INNER + gn], axis=-1)
    xs = xs.reshape(b, s, SSD_N_HEADS, SSD_HEAD_DIM).astype(jnp.float32)
    Bm = Bm.reshape(b, s, SSD_N_GROUPS, SSD_D_STATE).astype(jnp.float32)
    Cm = Cm.reshape(b, s, SSD_N_GROUPS, SSD_D_STATE).astype(jnp.float32)
    dt = jax.nn.softplus(dt_raw.astype(jnp.float32) + dt_bias.astype(jnp.float32))
    A = -jnp.exp(a_log.astype(jnp.float32))
    y = ssd_chunked(xs * dt[..., None], dt * A, Bm, Cm)
    y = (y + xs * d_skip.astype(jnp.float32)[:, None]).reshape(b, s, SSD_D_INNER)
    yg = (y * jax.nn.silu(z.astype(jnp.float32))).reshape(b, s, SSD_N_GROUPS, SSD_D_INNER // SSD_N_GROUPS)
    yg = yg * lax.rsqrt(jnp.mean(yg * yg, axis=-1, keepdims=True) + NORM_EPS)
    yg = yg.reshape(b, s, SSD_D_INNER) * norm_w.astype(jnp.float32)
    return yg.astype(z.dtype)


def diff_attn_branch(q, k, v, lq1, lk1, lq2, lk2, subln_w, lambda_init):
    b, s, _ = q.shape
    H, d = ATTN_N_HEADS, ATTN_HEAD_DIM
    cos, sin = rope_tables(s)
    q = apply_rope(q.reshape(b, s, H, 2, d), cos, sin) * (d ** -0.5)
    k = apply_rope(k.reshape(b, s, H, 2, d), cos, sin)
    qt = q.transpose(0, 2, 3, 1, 4)
    kt = k.transpose(0, 2, 3, 1, 4)
    vt = v.reshape(b, s, H, 2 * d).transpose(0, 2, 1, 3)
    f32 = jnp.float32
    lam = (jnp.exp(jnp.sum(lq1.astype(f32) * lk1.astype(f32)))
           - jnp.exp(jnp.sum(lq2.astype(f32) * lk2.astype(f32))) + lambda_init)
    kpos = jnp.arange(s)

    def query_block(i):
        qb = lax.dynamic_slice_in_dim(qt, i * Q_BLOCK, Q_BLOCK, axis=3)
        sc = jnp.einsum("bhiqd,bhikd->bhiqk", qb, kt).astype(f32)
        qpos = i * Q_BLOCK + jnp.arange(Q_BLOCK)
        sc = jnp.where(kpos[None, :] <= qpos[:, None], sc, -jnp.inf)
        p = jax.nn.softmax(sc, axis=-1)
        attn = p[:, :, 0] - lam * p[:, :, 1]
        return jnp.einsum("bhqk,bhke->bhqe", attn.astype(vt.dtype), vt)

    o = lax.map(query_block, jnp.arange(s // Q_BLOCK))
    o = o.transpose(1, 0, 3, 2, 4).reshape(b, s, H, 2 * d)
    o = rms_norm(o, subln_w, SUBLN_EPS) * (1.0 - lambda_init)
    return o.reshape(b, s, ATTN_WIDTH)


def conv_ffn(h, w_up, conv_w, conv_b, w_down):
    gate, val = jnp.split(h @ w_up, [D_FF], axis=-1)
    gate = causal_dwconv(gate, conv_w, conv_b)
    return (jax.nn.silu(gate) * val) @ w_down


def setup_inputs(seed: int = 0) -> dict:
    key = jax.random.key(seed)
    ks = jax.random.split(key, 24)
    f32 = jnp.float32

    def nrm(k, shape, scale):
        return jax.random.normal(k, shape, f32) * scale

    def gain(k, shape):
        return 1.0 + 0.01 * jax.random.normal(k, shape, f32)

    dt = jnp.exp(jax.random.uniform(ks[5], (DEPTH, SSD_N_HEADS), f32)
                 * (math.log(0.1) - math.log(0.001)) + math.log(0.001))
    dt = jnp.maximum(dt, 1e-4)
    dt_bias = dt + jnp.log(-jnp.expm1(-dt))
    return {
        "x": jax.random.normal(ks[0], (BATCH, SEQ, D_MODEL), f32),
        "norm_mix_w": gain(ks[1], (DEPTH, D_MODEL)),
        "w_in": nrm(ks[2], (DEPTH, D_MODEL, IN_COLS), D_MODEL ** -0.5),
        "ssd_conv_w": nrm(ks[3], (DEPTH, SSD_CONV_WIDTH, SSD_CONV_DIM), SSD_CONV_WIDTH ** -0.5),
        "ssd_conv_b": nrm(ks[4], (DEPTH, SSD_CONV_DIM), 0.01),
        "ssd_dt_bias": dt_bias,
        "ssd_a_log": jnp.log(jax.random.uniform(ks[6], (DEPTH, SSD_N_HEADS), f32, 1.0, 16.0)),
        "ssd_d_skip": gain(ks[7], (DEPTH, SSD_N_HEADS)),
        "ssd_norm_w": gain(ks[8], (DEPTH, SSD_D_INNER)),
        "lambda_q1": nrm(ks[9], (DEPTH, ATTN_HEAD_DIM), 0.1),
        "lambda_k1": nrm(ks[10], (DEPTH, ATTN_HEAD_DIM), 0.1),
        "lambda_q2": nrm(ks[11], (DEPTH, ATTN_HEAD_DIM), 0.1),
        "lambda_k2": nrm(ks[12], (DEPTH, ATTN_HEAD_DIM), 0.1),
        "subln_w": gain(ks[13], (DEPTH, 2 * ATTN_HEAD_DIM)),
        "w_branch_ssd": nrm(ks[14], (DEPTH, SSD_D_INNER, D_MODEL), SSD_D_INNER ** -0.5),
        "w_branch_attn": nrm(ks[15], (DEPTH, ATTN_WIDTH, D_MODEL), ATTN_WIDTH ** -0.5),
        "w_out": nrm(ks[16], (DEPTH, D_MODEL, D_MODEL), D_MODEL ** -0.5),
        "norm_ffn_w": gain(ks[17], (DEPTH, D_MODEL)),
        "w_up": nrm(ks[18], (DEPTH, D_MODEL, 2 * D_FF), D_MODEL ** -0.5),
        "ffn_conv_w": nrm(ks[19], (DEPTH, FFN_CONV_WIDTH, D_FF), FFN_CONV_WIDTH ** -0.5),
        "ffn_conv_b": nrm(ks[20], (DEPTH, D_FF), 0.01),
        "w_down": nrm(ks[21], (DEPTH, D_FF, D_MODEL), D_FF ** -0.5),
        "final_norm_w": gain(ks[22], (D_MODEL,)),
    }


def reference(x, norm_mix_w, w_in, ssd_conv_w, ssd_conv_b, ssd_dt_bias, ssd_a_log, ssd_d_skip,
              ssd_norm_w, lambda_q1, lambda_k1, lambda_q2, lambda_k2, subln_w, w_branch_ssd,
              w_branch_attn, w_out, norm_ffn_w, w_up, ffn_conv_w, ffn_conv_b, w_down, final_norm_w):
    split_pts = _split_points(IN_SIZES)
    for i in range(DEPTH):
        lambda_init = 0.8 - 0.6 * math.exp(-0.3 * i)
        h = rms_norm(x, norm_mix_w[i])
        proj = h @ w_in[i]
        z, xbc, dt_raw, q, k, v, g_ssd, g_attn = jnp.split(proj, split_pts, axis=-1)
        y_ssd = ssd_branch(z, xbc, dt_raw, ssd_conv_w[i], ssd_conv_b[i], ssd_dt_bias[i],
                           ssd_a_log[i], ssd_d_skip[i], ssd_norm_w[i])
        y_attn = diff_attn_branch(q, k, v, lambda_q1[i], lambda_k1[i], lambda_q2[i], lambda_k2[i],
                                  subln_w[i], lambda_init)
        merged = (jax.nn.sigmoid(g_ssd) * (y_ssd @ w_branch_ssd[i])
                  + jax.nn.sigmoid(g_attn) * (y_attn @ w_branch_attn[i]))
        x = x + merged @ w_out[i]
        x = x + conv_ffn(rms_norm(x, norm_ffn_w[i]), w_up[i], ffn_conv_w[i], ffn_conv_b[i], w_down[i])
    return rms_norm(x, final_norm_w)
```

```python
import functools
import math

import jax
import jax.numpy as jnp
from jax import lax
from jax.experimental import pallas as pl
from jax.experimental.pallas import tpu as pltpu

F32 = jnp.float32
BF16 = jnp.bfloat16

D_MODEL = 1024
SSD_HEAD_DIM = 64
SSD_N_HEADS = 16
SSD_N_GROUPS = 2
SSD_D_STATE = 128
SSD_CONV_WIDTH = 4
SSD_CHUNK = 128
SSD_CONV_DIM = D_MODEL + 2 * SSD_N_GROUPS * SSD_D_STATE
ATTN_HEAD_DIM = 64
ATTN_N_HEADS = 8
ROPE_THETA = 10000.0
D_FF = 2816
FFN_CONV_WIDTH = 3
NORM_EPS = 1e-6
SUBLN_EPS = 1e-5
LAMBDA_INIT = 0.8 - 0.6 * math.exp(-0.3 * 0)

LANES = 128
SUBLANES = 8
VMEM_LIMIT_BYTES = 56 * 1024 * 1024

_OFF_Z, _OFF_Q, _OFF_K, _OFF_V, _OFF_GS, _OFF_GA = (i * D_MODEL for i in range(6))
_OFF_XBC = 6 * D_MODEL
_OFF_DT = _OFF_XBC + SSD_CONV_DIM
_IN_COLS_PADDED = _OFF_DT + LANES


def _const_spec(shape):
    nd = len(shape)
    return pl.BlockSpec(shape, lambda *_: (0,) * nd, pipeline_mode=pl.Buffered(1))


def _rms(x, w, eps):
    return x * lax.rsqrt(jnp.mean(x * x, axis=-1, keepdims=True) + eps) * w


def _split2(x):
    hi = x.astype(BF16)
    lo = (x - hi.astype(F32)).astype(BF16)
    return hi, lo


def _split3(x):
    hi = x.astype(BF16)
    r = x - hi.astype(F32)
    mid = r.astype(BF16)
    lo = (r - mid.astype(F32)).astype(BF16)
    return hi, mid, lo


def _softplus(x):
    return jnp.maximum(x, 0.0) + jnp.log1p(jnp.exp(-jnp.abs(x)))


def _silu(x):
    return x * jax.nn.sigmoid(x)


def _inproj_kernel(x_ref, nw_ref, w_ref, wdt_ref, cos_ref, sa_ref, sb_ref,
                   z_ref, q_ref, k_ref, v_ref, gs_ref, ga_ref, xbc_ref, dt_ref, dtT_ref):
    h = _rms(x_ref[...], nw_ref[...], NORM_EPS).astype(BF16)

    def mm(lo, width):
        return jnp.dot(h, w_ref[:, lo:lo + width], preferred_element_type=F32)

    z_ref[...] = mm(_OFF_Z, D_MODEL).astype(BF16)
    v_ref[...] = mm(_OFF_V, D_MODEL).astype(BF16)
    gs_ref[...] = mm(_OFF_GS, D_MODEL).astype(BF16)
    ga_ref[...] = mm(_OFF_GA, D_MODEL).astype(BF16)
    xbc_ref[...] = mm(_OFF_XBC, SSD_CONV_DIM).astype(BF16)
    dt_ref[...] = mm(_OFF_DT, LANES)
    dtT_ref[...] = lax.dot_general(wdt_ref[...], h, (((1,), (1,)), ((), ())),
                                   preferred_element_type=F32)

    cos, sa, sb = cos_ref[...], sa_ref[...], sb_ref[...]
    half = ATTN_HEAD_DIM // 2

    def rope_store(off, out_ref, scale):
        acc = mm(off, D_MODEL)
        for j in range(D_MODEL // LANES):
            blk = acc[:, j * LANES:(j + 1) * LANES]
            r = (blk * cos + pltpu.roll(blk, half, 1) * sa
                 + pltpu.roll(blk, LANES - half, 1) * sb)
            if scale != 1.0:
                r = r * scale
            out_ref[:, j * LANES:(j + 1) * LANES] = r.astype(BF16)

    rope_store(_OFF_Q, q_ref, ATTN_HEAD_DIM ** -0.5)
    rope_store(_OFF_K, k_ref, 1.0)


def _in_projection(x2d, norm_w, w_all, w_dtT, cos_t, sa_t, sb_t, seq, tm):
    t, d = x2d.shape
    n_seq_tiles = seq // tm
    row = lambda i: (i, 0)
    tab = lambda i: (i % n_seq_tiles, 0)
    big = jax.ShapeDtypeStruct((t, D_MODEL), BF16)
    return pl.pallas_call(
        _inproj_kernel,
        grid=(t // tm,),
        in_specs=[
            pl.BlockSpec((tm, d), row),
            _const_spec((1, d)),
            _const_spec(w_all.shape),
            _const_spec(w_dtT.shape),
            pl.BlockSpec((tm, LANES), tab),
            pl.BlockSpec((tm, LANES), tab),
            pl.BlockSpec((tm, LANES), tab),
        ],
        out_specs=[
            pl.BlockSpec((tm, D_MODEL), row),
            pl.BlockSpec((tm, D_MODEL), row),
            pl.BlockSpec((tm, D_MODEL), row),
            pl.BlockSpec((tm, D_MODEL), row),
            pl.BlockSpec((tm, D_MODEL), row),
            pl.BlockSpec((tm, D_MODEL), row),
            pl.BlockSpec((tm, SSD_CONV_DIM), row),
            pl.BlockSpec((tm, LANES), row),
            pl.BlockSpec((SSD_N_HEADS, tm), lambda i: (0, i)),
        ],
        out_shape=[big, big, big, big, big, big,
                   jax.ShapeDtypeStruct((t, SSD_CONV_DIM), BF16),
                   jax.ShapeDtypeStruct((t, LANES), F32),
                   jax.ShapeDtypeStruct((SSD_N_HEADS, t), F32)],
        compiler_params=pltpu.CompilerParams(
            dimension_semantics=("arbitrary",), vmem_limit_bytes=VMEM_LIMIT_BYTES),
        name="norm_in_projection",
    )(x2d, norm_w, w_all, w_dtT, cos_t, sa_t, sb_t)


def _ssd_kernel(z_ref, xbc_ref, dt_ref, dtT_ref, cw_ref, cb_ref, dtb_row_ref, dtb_col_ref,
                alog_row_ref, alog_col_ref, dexp_ref, nw_ref, expand_ref,
                y_ref, state_ref, ext_ref):
    L = SSD_CHUNK
    P = SSD_HEAD_DIM
    N = SSD_D_STATE
    KW = SSD_CONV_WIDTH
    heads_per_group = SSD_N_HEADS // SSD_N_GROUPS
    c = pl.program_id(1)

    @pl.when(c == 0)
    def _():
        state_ref[...] = jnp.zeros_like(state_ref)
        ext_ref[0:SUBLANES, :] = jnp.zeros((SUBLANES, SSD_CONV_DIM), F32)

    xraw = xbc_ref[...].astype(F32)
    ext_ref[SUBLANES:SUBLANES + L, :] = xraw
    cw = cw_ref[...]
    conv = cb_ref[...] + cw[KW - 1:KW, :] * xraw
    for j in range(KW - 1):
        shift = KW - 1 - j
        conv = conv + cw[j:j + 1, :] * ext_ref[pl.ds(SUBLANES - shift, L), :]
    ext_ref[0:SUBLANES, :] = xraw[L - SUBLANES:, :]
    xc = _silu(conv)
    xs = xc[:, :D_MODEL]
    xs_b = xs.astype(BF16)

    dt = _softplus(dt_ref[...] + dtb_row_ref[...])
    dtT = _softplus(dtT_ref[...] + dtb_col_ref[...])
    dA = dt * (-jnp.exp(alog_row_ref[...]))
    dAT = dtT * (-jnp.exp(alog_col_ref[...]))
    ri = lax.broadcasted_iota(jnp.int32, (L, L), 0)
    ci = lax.broadcasted_iota(jnp.int32, (L, L), 1)
    causal = ri >= ci
    tri = jnp.where(causal, 1.0, 0.0).astype(BF16)
    triT = jnp.where(ri <= ci, 1.0, 0.0).astype(BF16)
    a_cs = sum(jnp.dot(tri, part, preferred_element_type=F32) for part in _split3(dA))
    a_csT = sum(jnp.dot(part, triT, preferred_element_type=F32) for part in _split3(dAT))
    a_last = a_cs[L - 1:L, :]
    exp_a = jnp.exp(a_cs)
    w_state = dt * jnp.exp(a_last - a_cs)

    expand = expand_ref[...]
    w_state_x = sum(jnp.dot(part, expand, preferred_element_type=F32) for part in _split2(w_state))
    a_tail = jnp.broadcast_to(a_last, (SUBLANES, LANES))
    a_last_x = sum(jnp.dot(part, expand, preferred_element_type=F32) for part in _split3(a_tail))[0:1, :]
    xd_b = (xs * w_state_x).astype(BF16)

    lane = lax.broadcasted_iota(jnp.int32, (L, LANES), 1)
    first_half = lane < P
    y_parts = []
    new_state_parts = []
    for g in range(SSD_N_GROUPS):
        b_g = xc[:, D_MODEL + g * N:D_MODEL + (g + 1) * N]
        c_g = xc[:, D_MODEL + SSD_N_GROUPS * N + g * N:D_MODEL + SSD_N_GROUPS * N + (g + 1) * N]
        b_gb = b_g.astype(BF16)
        c_gb = c_g.astype(BF16)
        cb = lax.dot_general(c_gb, b_gb, (((1,), (1,)), ((), ())), preferred_element_type=F32)
        for pair in range(heads_per_group // 2):
            col0 = (g * heads_per_group + 2 * pair) * P
            x_pair = xs_b[:, col0:col0 + 2 * P]
            prev_pair = state_ref[:, col0:col0 + 2 * P].astype(BF16)
            outs = []
            for k in range(2):
                hd = g * heads_per_group + 2 * pair + k
                seg = a_cs[:, hd:hd + 1] - a_csT[hd:hd + 1, :]
                decay = jnp.exp(jnp.where(causal, seg, -jnp.inf))
                m_h = (cb * decay * dtT[hd:hd + 1, :]).astype(BF16)
                c_h = (c_g * exp_a[:, hd:hd + 1]).astype(BF16)
                outs.append(jnp.dot(m_h, x_pair, preferred_element_type=F32)
                            + jnp.dot(c_h, prev_pair, preferred_element_type=F32))
            y_parts.append(jnp.where(first_half, outs[0], outs[1]))
        lo = g * heads_per_group * P
        hi = (g + 1) * heads_per_group * P
        new_state_parts.append(lax.dot_general(b_gb, xd_b[:, lo:hi], (((0,), (0,)), ((), ())),
                                               preferred_element_type=F32))
    state_ref[...] = state_ref[...] * jnp.exp(a_last_x) + jnp.concatenate(new_state_parts, axis=1)

    y = jnp.concatenate(y_parts, axis=1) + xs * dexp_ref[...]
    yg = y * _silu(z_ref[...].astype(F32))
    gw = D_MODEL // SSD_N_GROUPS
    nw = nw_ref[...]
    for g in range(SSD_N_GROUPS):
        part = yg[:, g * gw:(g + 1) * gw]
        y_ref[:, g * gw:(g + 1) * gw] = _rms(part, nw[:, g * gw:(g + 1) * gw], NORM_EPS).astype(BF16)


def _ssd_branch(z, xbc, dt, dtT, conv_w, conv_b, dt_bias, a_log, d_skip, norm_w, batch, seq):
    L = SSD_CHUNK
    nc = seq // L
    t = batch * seq
    pad = LANES - SSD_N_HEADS
    dtb_row = jnp.pad(dt_bias, (0, pad)).reshape(1, LANES)
    alog_row = jnp.pad(a_log, (0, pad)).reshape(1, LANES)
    dtb_col = dt_bias.reshape(SSD_N_HEADS, 1)
    alog_col = a_log.reshape(SSD_N_HEADS, 1)
    dexp = jnp.repeat(d_skip, SSD_HEAD_DIM).reshape(1, D_MODEL)
    expand = (jnp.arange(LANES)[:, None] == (jnp.arange(D_MODEL)[None, :] // SSD_HEAD_DIM)).astype(BF16)
    row = lambda b, c: (b * nc + c, 0)
    return pl.pallas_call(
        _ssd_kernel,
        grid=(batch, nc),
        in_specs=[
            pl.BlockSpec((L, D_MODEL), row),
            pl.BlockSpec((L, SSD_CONV_DIM), row),
            pl.BlockSpec((L, LANES), row),
            pl.BlockSpec((SSD_N_HEADS, L), lambda b, c: (0, b * nc + c)),
            _const_spec((SSD_CONV_WIDTH, SSD_CONV_DIM)),
            _const_spec((1, SSD_CONV_DIM)),
            _const_spec((1, LANES)),
            _const_spec((SSD_N_HEADS, 1)),
            _const_spec((1, LANES)),
            _const_spec((SSD_N_HEADS, 1)),
            _const_spec((1, D_MODEL)),
            _const_spec((1, D_MODEL)),
            _const_spec((LANES, D_MODEL)),
        ],
        out_specs=pl.BlockSpec((L, D_MODEL), row),
        out_shape=jax.ShapeDtypeStruct((t, D_MODEL), BF16),
        scratch_shapes=[
            pltpu.VMEM((SSD_D_STATE, D_MODEL), F32),
            pltpu.VMEM((SUBLANES + L, SSD_CONV_DIM), F32),
        ],
        compiler_params=pltpu.CompilerParams(
            dimension_semantics=("arbitrary", "arbitrary"), vmem_limit_bytes=VMEM_LIMIT_BYTES),
        name="ssd_branch",
    )(z, xbc, dt, dtT, conv_w, conv_b.reshape(1, -1), dtb_row, dtb_col, alog_row, alog_col,
      dexp, norm_w.reshape(1, -1), expand)


def _attn_kernel(q_ref, k_ref, v_ref, lq1_ref, lk1_ref, lq2_ref, lk2_ref, sw_ref, o_ref, *, tq, tk):
    i = pl.program_id(2)
    d = ATTN_HEAD_DIM
    q = q_ref[...]
    lane = lax.broadcasted_iota(jnp.int32, (tq, LANES), 1)
    zero = jnp.zeros_like(q)
    q_maps = (jnp.where(lane < d, q, zero), jnp.where(lane >= d, q, zero))
    rows = lax.broadcasted_iota(jnp.int32, (tq, tk), 0)
    cols = lax.broadcasted_iota(jnp.int32, (tq, tk), 1)
    blocks_per_q = tq // tk

    def step(j, carry, masked):
        kb = k_ref[pl.ds(pl.multiple_of(j * tk, tk), tk), :]
        vb = v_ref[pl.ds(pl.multiple_of(j * tk, tk), tk), :]
        new = []
        for mp in range(2):
            m_prev, l_prev, acc_prev = carry[mp]
            s = lax.dot_general(q_maps[mp], kb, (((1,), (1,)), ((), ())), preferred_element_type=F32)
            if masked:
                s = jnp.where(cols + j * tk <= rows + i * tq, s, -jnp.inf)
            m_new = jnp.maximum(m_prev, jnp.max(s, axis=-1, keepdims=True))
            alpha = jnp.exp(m_prev - m_new)
            p = jnp.exp(s - m_new)
            l_new = alpha * l_prev + jnp.sum(p, axis=-1, keepdims=True)
            acc_new = alpha * acc_prev + jnp.dot(p.astype(BF16), vb, preferred_element_type=F32)
            new.append((m_new, l_new, acc_new))
        return tuple(new)

    init_one = (jnp.full((tq, 1), -jnp.inf, F32), jnp.zeros((tq, 1), F32), jnp.zeros((tq, LANES), F32))
    carry = (init_one, init_one)
    n_full = i * blocks_per_q
    carry = lax.fori_loop(0, n_full, lambda j, cr: step(j, cr, False), carry)
    for jj in range(blocks_per_q):
        carry = step(n_full + jj, carry, True)

    lam = (jnp.exp(jnp.sum(lq1_ref[...] * lk1_ref[...], axis=-1, keepdims=True))
           - jnp.exp(jnp.sum(lq2_ref[...] * lk2_ref[...], axis=-1, keepdims=True)) + LAMBDA_INIT)
    (_, l0, acc0), (_, l1, acc1) = carry
    o = acc0 / l0 - lam * (acc1 / l1)
    o_ref[...] = (_rms(o, sw_ref[...], SUBLN_EPS) * (1.0 - LAMBDA_INIT)).astype(BF16)


def _diff_attention(q, k, v, lq1, lk1, lq2, lk2, subln_w, batch, seq, tq, tk):
    t = batch * seq
    nq = seq // tq
    vec = lambda a: a.reshape(1, -1)
    return pl.pallas_call(
        functools.partial(_attn_kernel, tq=tq, tk=tk),
        grid=(batch, ATTN_N_HEADS, nq),
        in_specs=[
            pl.BlockSpec((tq, LANES), lambda b, h, i: (b * nq + i, h)),
            pl.BlockSpec((seq, LANES), lambda b, h, i: (b, h)),
            pl.BlockSpec((seq, LANES), lambda b, h, i: (b, h)),
            _const_spec((1, ATTN_HEAD_DIM)),
            _const_spec((1, ATTN_HEAD_DIM)),
            _const_spec((1, ATTN_HEAD_DIM)),
            _const_spec((1, ATTN_HEAD_DIM)),
            _const_spec((1, 2 * ATTN_HEAD_DIM)),
        ],
        out_specs=pl.BlockSpec((tq, LANES), lambda b, h, i: (b * nq + i, h)),
        out_shape=jax.ShapeDtypeStruct((t, D_MODEL), BF16),
        compiler_params=pltpu.CompilerParams(
            dimension_semantics=("arbitrary", "arbitrary", "arbitrary"),
            vmem_limit_bytes=VMEM_LIMIT_BYTES),
        name="diff_attention",
    )(q, k, v, vec(lq1), vec(lk1), vec(lq2), vec(lk2), vec(subln_w))


def _merge_kernel(x_ref, ys_ref, ya_ref, gs_ref, ga_ref, wbs_ref, wba_ref, wo_ref, o_ref):
    a = jnp.dot(ys_ref[...], wbs_ref[...], preferred_element_type=F32)
    b = jnp.dot(ya_ref[...], wba_ref[...], preferred_element_type=F32)
    merged = (jax.nn.sigmoid(gs_ref[...].astype(F32)) * a
              + jax.nn.sigmoid(ga_ref[...].astype(F32)) * b)
    o_ref[...] = x_ref[...] + jnp.dot(merged.astype(BF16), wo_ref[...], preferred_element_type=F32)


def _merge(x2d, ys, ya, gs, ga, wbs, wba, wo, tm):
    t, d = x2d.shape
    row = lambda i: (i, 0)
    tile = pl.BlockSpec((tm, d), row)
    return pl.pallas_call(
        _merge_kernel,
        grid=(t // tm,),
        in_specs=[tile, tile, tile, tile, tile,
                  _const_spec((d, d)), _const_spec((d, d)), _const_spec((d, d))],
        out_specs=tile,
        out_shape=jax.ShapeDtypeStruct((t, d), F32),
        compiler_params=pltpu.CompilerParams(
            dimension_semantics=("arbitrary",), vmem_limit_bytes=VMEM_LIMIT_BYTES),
        name="gated_merge_out_projection",
    )(x2d, ys, ya, gs, ga, wbs, wba, wo)


def _ffn_kernel(x_ref, halo_ref, nw_ref, wup_ref, cw_ref, cb_ref, wd_ref, fw_ref, o_ref, gbuf_ref,
                *, tm, tiles_per_seq, chunk):
    i = pl.program_id(0)
    x1 = x_ref[...]
    nw = nw_ref[...]
    h2 = _rms(x1, nw, NORM_EPS).astype(BF16)
    hh = _rms(halo_ref[...], nw, NORM_EPS).astype(BF16)
    keep_halo = jnp.where(i % tiles_per_seq == 0, 0.0, 1.0)
    cw = cw_ref[...]
    cb = cb_ref[...]
    KW = FFN_CONV_WIDTH
    acc = jnp.zeros((tm, D_MODEL), F32)
    for c in range(D_FF // chunk):
        lo = c * chunk
        wg = wup_ref[:, lo:lo + chunk]
        gate = jnp.dot(h2, wg, preferred_element_type=F32)
        val = jnp.dot(h2, wup_ref[:, D_FF + lo:D_FF + lo + chunk], preferred_element_type=F32)
        gbuf_ref[0:SUBLANES, :] = jnp.dot(hh, wg, preferred_element_type=F32) * keep_halo
        gbuf_ref[SUBLANES:SUBLANES + tm, :] = gate
        conv = cb[:, lo:lo + chunk] + cw[KW - 1:KW, lo:lo + chunk] * gate
        for j in range(KW - 1):
            shift = KW - 1 - j
            conv = conv + cw[j:j + 1, lo:lo + chunk] * gbuf_ref[pl.ds(SUBLANES - shift, tm), :]
        act = (_silu(conv) * val).astype(BF16)
        acc = acc + jnp.dot(act, wd_ref[lo:lo + chunk, :], preferred_element_type=F32)
    o_ref[...] = _rms(x1 + acc, fw_ref[...], NORM_EPS)


def _conv_ffn(x1, norm_w, w_up, conv_w, conv_b, w_down, final_w, seq, tm, chunk):
    t, d = x1.shape
    tiles_per_seq = seq // tm
    halo_blocks = tm // SUBLANES
    return pl.pallas_call(
        functools.partial(_ffn_kernel, tm=tm, tiles_per_seq=tiles_per_seq, chunk=chunk),
        grid=(t // tm,),
        in_specs=[
            pl.BlockSpec((tm, d), lambda i: (i, 0)),
            pl.BlockSpec((SUBLANES, d), lambda i: (jnp.maximum(i * halo_blocks - 1, 0), 0)),
            _const_spec((1, d)),
            _const_spec(w_up.shape),
            _const_spec(conv_w.shape),
            _const_spec((1, D_FF)),
            _const_spec(w_down.shape),
            _const_spec((1, d)),
        ],
        out_specs=pl.BlockSpec((tm, d), lambda i: (i, 0)),
        out_shape=jax.ShapeDtypeStruct((t, d), F32),
        scratch_shapes=[pltpu.VMEM((SUBLANES + tm, chunk), F32)],
        compiler_params=pltpu.CompilerParams(
            dimension_semantics=("arbitrary",), vmem_limit_bytes=VMEM_LIMIT_BYTES),
        name="conv_ffn_final_norm",
    )(x1, x1, norm_w.reshape(1, -1), w_up, conv_w, conv_b.reshape(1, -1), w_down, final_w.reshape(1, -1))


def _rope_tables(seq):
    half = ATTN_HEAD_DIM // 2
    inv = 1.0 / (ROPE_THETA ** (jnp.arange(0, ATTN_HEAD_DIM, 2, dtype=F32) / ATTN_HEAD_DIM))
    ang = jnp.arange(seq, dtype=F32)[:, None] * inv[None, :]
    lane = jnp.arange(LANES)
    cos = jnp.cos(ang)[:, lane % half]
    sin = jnp.sin(ang)[:, lane % half]
    second = (lane % ATTN_HEAD_DIM) >= half
    sa = jnp.where(second[None, :], sin, 0.0)
    sb = jnp.where(second[None, :], 0.0, -sin)
    return cos, sa, sb


def _tile(n, pref):
    while n % pref:
        pref //= 2
    return pref


def kernel(x, norm_mix_w, w_in, ssd_conv_w, ssd_conv_b, ssd_dt_bias, ssd_a_log, ssd_d_skip, ssd_norm_w, lambda_q1, lambda_k1, lambda_q2, lambda_k2, subln_w, w_branch_ssd, w_branch_attn, w_out, norm_ffn_w, w_up, ffn_conv_w, ffn_conv_b, w_down, final_norm_w):
    batch, seq, d = x.shape
    assert d == D_MODEL and seq % SSD_CHUNK == 0
    t = batch * seq
    x2d = x.reshape(t, d)
    li = 0

    w = w_in[li]
    o_z, o_xbc = 0, D_MODEL
    o_dt = o_xbc + SSD_CONV_DIM
    o_q = o_dt + SSD_N_HEADS
    cols = lambda lo, n: w[:, lo:lo + n]
    w_dt = cols(o_dt, SSD_N_HEADS)
    w_all = jnp.concatenate([
        cols(o_z, D_MODEL), cols(o_q, D_MODEL), cols(o_q + D_MODEL, D_MODEL), cols(o_q + 2 * D_MODEL, D_MODEL),
        cols(o_q + 3 * D_MODEL, D_MODEL), cols(o_q + 4 * D_MODEL, D_MODEL), cols(o_xbc, SSD_CONV_DIM),
        jnp.pad(w_dt, ((0, 0), (0, LANES - SSD_N_HEADS)))], axis=1).astype(BF16)
    assert w_all.shape[1] == _IN_COLS_PADDED
    w_dtT = w_dt.T.astype(BF16)

    tm = _tile(seq, 512)
    cos_t, sa_t, sb_t = _rope_tables(seq)
    z, q, k, v, gs, ga, xbc, dt, dtT = _in_projection(
        x2d, norm_mix_w[li].reshape(1, -1), w_all, w_dtT, cos_t, sa_t, sb_t, seq, tm)

    y_ssd = _ssd_branch(z, xbc, dt, dtT, ssd_conv_w[li], ssd_conv_b[li], ssd_dt_bias[li], ssd_a_log[li],
                        ssd_d_skip[li], ssd_norm_w[li], batch, seq)
    tq = _tile(seq, 512)
    y_attn = _diff_attention(q, k, v, lambda_q1[li], lambda_k1[li], lambda_q2[li], lambda_k2[li],
                             subln_w[li], batch, seq, tq, tq)
    x1 = _merge(x2d, y_ssd, y_attn, gs, ga, w_branch_ssd[li].astype(BF16), w_branch_attn[li].astype(BF16),
                w_out[li].astype(BF16), tm)
    out = _conv_ffn(x1, norm_ffn_w[li], w_up[li].astype(BF16), ffn_conv_w[li], ffn_conv_b[li],
                    w_down[li].astype(BF16), final_norm_w, seq, tm, 256)
    return out.reshape(batch, seq, d)
```

```python
import functools
import math

import jax
import jax.numpy as jnp
from jax import lax
from jax.experimental import pallas as pl
from jax.experimental.pallas import tpu as pltpu

F32 = jnp.float32
BF16 = jnp.bfloat16

D_MODEL = 1024
SSD_HEAD_DIM = 64
SSD_N_HEADS = 16
SSD_N_GROUPS = 2
SSD_D_STATE = 128
SSD_CONV_WIDTH = 4
SSD_CHUNK = 128
SSD_CONV_DIM = D_MODEL + 2 * SSD_N_GROUPS * SSD_D_STATE
ATTN_HEAD_DIM = 64
ATTN_N_HEADS = 8
ROPE_THETA = 10000.0
D_FF = 2816
FFN_CONV_WIDTH = 3
NORM_EPS = 1e-6
SUBLN_EPS = 1e-5
LAMBDA_INIT = 0.8 - 0.6 * math.exp(-0.3 * 0)

LANES = 128
SUBLANES = 8
ONES_ROWS = 16
FFN_HALO = 16
VMEM_LIMIT_BYTES = 56 * 1024 * 1024

_OFF_Z, _OFF_K, _OFF_GS, _OFF_GA = (i * D_MODEL for i in range(4))
_OFF_XBC = 4 * D_MODEL
_OFF_DT = _OFF_XBC + SSD_CONV_DIM
_IN_COLS_PADDED = _OFF_DT + LANES


def _const_spec(shape):
    nd = len(shape)
    return pl.BlockSpec(shape, lambda *_: (0,) * nd, pipeline_mode=pl.Buffered(1))


def _rms(x, w, eps):
    return x * lax.rsqrt(jnp.mean(x * x, axis=-1, keepdims=True) + eps) * w


def _split2(x):
    hi = x.astype(BF16)
    lo = (x - hi.astype(F32)).astype(BF16)
    return hi, lo


def _split3(x):
    hi = x.astype(BF16)
    r = x - hi.astype(F32)
    mid = r.astype(BF16)
    lo = (r - mid.astype(F32)).astype(BF16)
    return hi, mid, lo


def _softplus(x):
    return jnp.maximum(x, 0.0) + jnp.log1p(jnp.exp(-jnp.abs(x)))


def _silu(x):
    return x * jax.nn.sigmoid(x)


def _inproj_kernel(x_ref, nw_ref, w_ref, wqT_ref, wvT_ref, wdt_ref, cos_ref, sa_ref, sb_ref, cosT_ref, sinT_ref,
                   z_ref, qT_ref, k_ref, vT_ref, gs_ref, ga_ref, xbc_ref, dt_ref, dtT_ref):
    h = _rms(x_ref[...], nw_ref[...], NORM_EPS).astype(BF16)
    nt = (((1,), (1,)), ((), ()))

    def mm(lo, width):
        return jnp.dot(h, w_ref[:, lo:lo + width], preferred_element_type=F32)

    z_ref[...] = mm(_OFF_Z, D_MODEL).astype(BF16)
    gs_ref[...] = mm(_OFF_GS, D_MODEL).astype(BF16)
    ga_ref[...] = mm(_OFF_GA, D_MODEL).astype(BF16)
    xbc_ref[...] = mm(_OFF_XBC, SSD_CONV_DIM).astype(BF16)
    dt_ref[...] = mm(_OFF_DT, LANES)
    dtT_ref[...] = lax.dot_general(wdt_ref[...], h, nt, preferred_element_type=F32)
    vT_ref[...] = lax.dot_general(wvT_ref[...], h, nt, preferred_element_type=F32).astype(BF16)

    cos, sa, sb = cos_ref[...], sa_ref[...], sb_ref[...]
    half = ATTN_HEAD_DIM // 2
    acc = mm(_OFF_K, D_MODEL)
    for j in range(D_MODEL // LANES):
        blk = acc[:, j * LANES:(j + 1) * LANES]
        r = blk * cos + pltpu.roll(blk, half, 1) * sa + pltpu.roll(blk, LANES - half, 1) * sb
        k_ref[:, j * LANES:(j + 1) * LANES] = r.astype(BF16)

    scale = ATTN_HEAD_DIM ** -0.5 * math.log2(math.e)
    cosT, sinT = cosT_ref[...] * scale, sinT_ref[...] * scale
    accT = lax.dot_general(wqT_ref[...], h, nt, preferred_element_type=F32)
    for g in range(D_MODEL // ATTN_HEAD_DIM):
        lo = g * ATTN_HEAD_DIM
        x1 = accT[lo:lo + half, :]
        x2 = accT[lo + half:lo + 2 * half, :]
        qT_ref[lo:lo + half, :] = (x1 * cosT - x2 * sinT).astype(BF16)
        qT_ref[lo + half:lo + 2 * half, :] = (x2 * cosT + x1 * sinT).astype(BF16)


def _in_projection(x2d, norm_w, w_all, w_qT, w_vT, w_dtT, rope, seq, tm):
    t, d = x2d.shape
    n_seq_tiles = seq // tm
    cos_t, sa_t, sb_t, cosT_t, sinT_t = rope
    half = ATTN_HEAD_DIM // 2
    row = lambda i: (i, 0)
    colblk = lambda i: (0, i)
    tab = lambda i: (i % n_seq_tiles, 0)
    tabT = lambda i: (0, i % n_seq_tiles)
    big = jax.ShapeDtypeStruct((t, D_MODEL), BF16)
    bigT = jax.ShapeDtypeStruct((D_MODEL, t), BF16)
    return pl.pallas_call(
        _inproj_kernel,
        grid=(t // tm,),
        in_specs=[
            pl.BlockSpec((tm, d), row),
            _const_spec((1, d)),
            _const_spec(w_all.shape),
            _const_spec(w_qT.shape),
            _const_spec(w_vT.shape),
            _const_spec(w_dtT.shape),
            pl.BlockSpec((tm, LANES), tab),
            pl.BlockSpec((tm, LANES), tab),
            pl.BlockSpec((tm, LANES), tab),
            pl.BlockSpec((half, tm), tabT),
            pl.BlockSpec((half, tm), tabT),
        ],
        out_specs=[
            pl.BlockSpec((tm, D_MODEL), row),
            pl.BlockSpec((D_MODEL, tm), colblk),
            pl.BlockSpec((tm, D_MODEL), row),
            pl.BlockSpec((D_MODEL, tm), colblk),
            pl.BlockSpec((tm, D_MODEL), row),
            pl.BlockSpec((tm, D_MODEL), row),
            pl.BlockSpec((tm, SSD_CONV_DIM), row),
            pl.BlockSpec((tm, LANES), row),
            pl.BlockSpec((SSD_N_HEADS, tm), colblk),
        ],
        out_shape=[big, bigT, big, bigT, big, big,
                   jax.ShapeDtypeStruct((t, SSD_CONV_DIM), BF16),
                   jax.ShapeDtypeStruct((t, LANES), F32),
                   jax.ShapeDtypeStruct((SSD_N_HEADS, t), F32)],
        compiler_params=pltpu.CompilerParams(
            dimension_semantics=("arbitrary",), vmem_limit_bytes=VMEM_LIMIT_BYTES),
        name="norm_in_projection",
    )(x2d, norm_w, w_all, w_qT, w_vT, w_dtT, cos_t, sa_t, sb_t, cosT_t, sinT_t)


def _ssd_kernel(z_ref, xbc_ref, dt_ref, dtT_ref, cw_ref, cb_ref, dtb_row_ref, dtb_col_ref,
                alog_row_ref, alog_col_ref, dexp_ref, nw_ref, expand_ref,
                y_ref, state_ref, ext_ref):
    L = SSD_CHUNK
    P = SSD_HEAD_DIM
    N = SSD_D_STATE
    KW = SSD_CONV_WIDTH
    heads_per_group = SSD_N_HEADS // SSD_N_GROUPS
    c = pl.program_id(1)

    @pl.when(c == 0)
    def _():
        state_ref[...] = jnp.zeros_like(state_ref)
        ext_ref[...] = jnp.zeros((SUBLANES, SSD_CONV_DIM), F32)

    xraw = xbc_ref[...].astype(F32)
    x_ext = jnp.concatenate([ext_ref[...], xraw], axis=0)
    x_tiles = x_ext.reshape(L // SUBLANES + 1, SUBLANES, SSD_CONV_DIM)
    row_in_tile = lax.broadcasted_iota(jnp.int32, (L // SUBLANES, SUBLANES, SSD_CONV_DIM), 1)
    cw = cw_ref[...]
    conv = cb_ref[...] + cw[KW - 1:KW, :] * xraw
    for j in range(KW - 1):
        shift = KW - 1 - j
        rot = pltpu.roll(x_tiles, shift, 1)
        shifted = jnp.where(row_in_tile < shift, rot[:-1], rot[1:]).reshape(L, SSD_CONV_DIM)
        conv = conv + cw[j:j + 1, :] * shifted
    ext_ref[...] = xraw[L - SUBLANES:, :]
    xc = _silu(conv)
    xs = xc[:, :D_MODEL]
    xs_b = xs.astype(BF16)

    dt = _softplus(dt_ref[...] + dtb_row_ref[...])
    dtT = _softplus(dtT_ref[...] + dtb_col_ref[...])
    dA = dt * (-jnp.exp(alog_row_ref[...]))
    dAT = dtT * (-jnp.exp(alog_col_ref[...]))
    ri = lax.broadcasted_iota(jnp.int32, (L, L), 0)
    ci = lax.broadcasted_iota(jnp.int32, (L, L), 1)
    causal = ri >= ci
    tri = jnp.where(causal, 1.0, 0.0).astype(BF16)
    triT = jnp.where(ri <= ci, 1.0, 0.0).astype(BF16)
    a_cs = sum(jnp.dot(tri, part, preferred_element_type=F32) for part in _split3(dA))
    a_csT = sum(jnp.dot(part, triT, preferred_element_type=F32) for part in _split3(dAT))
    a_last = a_cs[L - 1:L, :]
    exp_a = jnp.exp(a_cs)
    w_state = dt * jnp.exp(a_last - a_cs)

    expand = expand_ref[...]
    w_state_x = sum(jnp.dot(part, expand, preferred_element_type=F32) for part in _split2(w_state))
    a_tail = jnp.broadcast_to(a_last, (SUBLANES, LANES))
    a_last_x = sum(jnp.dot(part, expand, preferred_element_type=F32) for part in _split3(a_tail))[0:1, :]
    xd_b = (xs * w_state_x).astype(BF16)

    lane = lax.broadcasted_iota(jnp.int32, (L, LANES), 1)
    first_half = lane < P
    y_parts = []
    new_state_parts = []
    for g in range(SSD_N_GROUPS):
        b_g = xc[:, D_MODEL + g * N:D_MODEL + (g + 1) * N]
        c_g = xc[:, D_MODEL + SSD_N_GROUPS * N + g * N:D_MODEL + SSD_N_GROUPS * N + (g + 1) * N]
        b_gb = b_g.astype(BF16)
        c_gb = c_g.astype(BF16)
        cb = lax.dot_general(c_gb, b_gb, (((1,), (1,)), ((), ())), preferred_element_type=F32)
        for pair in range(heads_per_group // 2):
            col0 = (g * heads_per_group + 2 * pair) * P
            x_pair = xs_b[:, col0:col0 + 2 * P]
            prev_pair = state_ref[:, col0:col0 + 2 * P].astype(BF16)
            outs = []
            for k in range(2):
                hd = g * heads_per_group + 2 * pair + k
                seg = a_cs[:, hd:hd + 1] - a_csT[hd:hd + 1, :]
                decay = jnp.exp(jnp.where(causal, seg, -jnp.inf))
                m_h = (cb * decay * dtT[hd:hd + 1, :]).astype(BF16)
                c_h = (c_g * exp_a[:, hd:hd + 1]).astype(BF16)
                outs.append(jnp.dot(m_h, x_pair, preferred_element_type=F32)
                            + jnp.dot(c_h, prev_pair, preferred_element_type=F32))
            y_parts.append(jnp.where(first_half, outs[0], outs[1]))
        lo = g * heads_per_group * P
        hi = (g + 1) * heads_per_group * P
        new_state_parts.append(lax.dot_general(b_gb, xd_b[:, lo:hi], (((0,), (0,)), ((), ())),
                                               preferred_element_type=F32))
    state_ref[...] = state_ref[...] * jnp.exp(a_last_x) + jnp.concatenate(new_state_parts, axis=1)

    y = jnp.concatenate(y_parts, axis=1) + xs * dexp_ref[...]
    yg = y * _silu(z_ref[...].astype(F32))
    gw = D_MODEL // SSD_N_GROUPS
    nw = nw_ref[...]
    for g in range(SSD_N_GROUPS):
        part = yg[:, g * gw:(g + 1) * gw]
        y_ref[:, g * gw:(g + 1) * gw] = _rms(part, nw[:, g * gw:(g + 1) * gw], NORM_EPS).astype(BF16)


def _ssd_branch(z, xbc, dt, dtT, conv_w, conv_b, dt_bias, a_log, d_skip, norm_w, batch, seq):
    L = SSD_CHUNK
    nc = seq // L
    t = batch * seq
    pad = LANES - SSD_N_HEADS
    dtb_row = jnp.pad(dt_bias, (0, pad)).reshape(1, LANES)
    alog_row = jnp.pad(a_log, (0, pad)).reshape(1, LANES)
    dtb_col = dt_bias.reshape(SSD_N_HEADS, 1)
    alog_col = a_log.reshape(SSD_N_HEADS, 1)
    dexp = jnp.repeat(d_skip, SSD_HEAD_DIM).reshape(1, D_MODEL)
    expand = (jnp.arange(LANES)[:, None] == (jnp.arange(D_MODEL)[None, :] // SSD_HEAD_DIM)).astype(BF16)
    row = lambda b, c: (b * nc + c, 0)
    return pl.pallas_call(
        _ssd_kernel,
        grid=(batch, nc),
        in_specs=[
            pl.BlockSpec((L, D_MODEL), row),
            pl.BlockSpec((L, SSD_CONV_DIM), row),
            pl.BlockSpec((L, LANES), row),
            pl.BlockSpec((SSD_N_HEADS, L), lambda b, c: (0, b * nc + c)),
            _const_spec((SSD_CONV_WIDTH, SSD_CONV_DIM)),
            _const_spec((1, SSD_CONV_DIM)),
            _const_spec((1, LANES)),
            _const_spec((SSD_N_HEADS, 1)),
            _const_spec((1, LANES)),
            _const_spec((SSD_N_HEADS, 1)),
            _const_spec((1, D_MODEL)),
            _const_spec((1, D_MODEL)),
            _const_spec((LANES, D_MODEL)),
        ],
        out_specs=pl.BlockSpec((L, D_MODEL), row),
        out_shape=jax.ShapeDtypeStruct((t, D_MODEL), BF16),
        scratch_shapes=[
            pltpu.VMEM((SSD_D_STATE, D_MODEL), F32),
            pltpu.VMEM((SUBLANES, SSD_CONV_DIM), F32),
        ],
        compiler_params=pltpu.CompilerParams(
            dimension_semantics=("arbitrary", "arbitrary"), vmem_limit_bytes=VMEM_LIMIT_BYTES),
        name="ssd_branch",
    )(z, xbc, dt, dtT, conv_w, conv_b.reshape(1, -1), dtb_row, dtb_col, alog_row, alog_col,
      dexp, norm_w.reshape(1, -1), expand)


def _attn_kernel(qT_ref, k_ref, vT_ref, lq1_ref, lk1_ref, lq2_ref, lk2_ref, swc_ref, o_ref,
                 s_ref, p_ref, alpha_ref, m_ref, acc_ref, *, seq, tq):
    d = ATTN_HEAD_DIM
    dv = 2 * d
    tk = tq
    nq = seq // tq
    n_off = nq * (nq - 1) // 2
    assert n_off % 2 == 0 and nq % 2 == 0, "step pairs need even counts"
    neg_inf = -jnp.inf

    p_ref[...] = jnp.zeros(p_ref.shape, BF16)
    alpha_ref[...] = jnp.ones(alpha_ref.shape, F32)
    m_ref[...] = jnp.full(m_ref.shape, neg_inf, F32)
    acc_ref[...] = jnp.zeros(acc_ref.shape, F32)
    key_i = lax.broadcasted_iota(jnp.int32, (tk, LANES), 0)
    qry_i = lax.broadcasted_iota(jnp.int32, (tk, LANES), 1)
    ones_rows = jnp.ones((ONES_ROWS, tk), BF16)
    feat = lax.broadcasted_iota(jnp.int32, (2 * d, tq), 0)
    map_rows = (feat < d, feat >= d)
    lam = (jnp.exp(jnp.sum(lq1_ref[...] * lk1_ref[...], axis=-1, keepdims=True))
           - jnp.exp(jnp.sum(lq2_ref[...] * lk2_ref[...], axis=-1, keepdims=True)) + LAMBDA_INIT)
    sw_col = swc_ref[...]

    def stage_scores(step, slot):
        i, j = step
        qT = qT_ref[:, pl.ds(pl.multiple_of(i * tq, tq), tq)]
        kb = k_ref[pl.ds(pl.multiple_of(j * tk, tk), tk), :]
        zero = jnp.zeros_like(qT)
        for mp in range(2):
            s_ref[slot, mp] = jnp.dot(kb, jnp.where(map_rows[mp], qT, zero), preferred_element_type=F32)

    def stage_softmax(step, slot, diagonal):
        i, _ = step
        for mp in range(2):
            for c in range(tq // LANES):
                cs = slice(c * LANES, (c + 1) * LANES)
                ss = s_ref[slot, mp, :, cs]
                if diagonal:
                    ss = jnp.where(key_i <= qry_i + c * LANES, ss, neg_inf)
                m_prev = m_ref[i, mp, :, cs]
                m_new = jnp.maximum(m_prev, jnp.max(ss, axis=0, keepdims=True))
                p_ref[slot, mp, :, cs] = jnp.exp2(ss - m_new).astype(BF16)
                alpha_ref[slot, mp, :, cs] = jnp.exp2(m_prev - m_new)
                m_ref[i, mp, :, cs] = m_new

    def stage_values(step, slot):
        i, j = step
        vT = vT_ref[:, pl.ds(pl.multiple_of(j * tk, tk), tk)]
        v_ext = jnp.concatenate([vT, ones_rows], axis=0)
        for mp in range(2):
            acc_ref[i, mp] = (alpha_ref[slot, mp] * acc_ref[i, mp]
                              + jnp.dot(v_ext, p_ref[slot, mp], preferred_element_type=F32))

    def following(step):
        i, j = step
        on_diag = j == i
        same_row = j + 1 < i
        off_i = jnp.where(same_row, i, i + 1)
        off_j = jnp.where(same_row, j + 1, 0)
        to_diag = off_i >= nq
        off_i = jnp.where(to_diag, 0, off_i)
        off_j = jnp.where(to_diag, 0, off_j)
        diag_i = jnp.minimum(i + 1, nq - 1)
        return jnp.where(on_diag, diag_i, off_i), jnp.where(on_diag, diag_i, off_j)

    first = (jnp.int32(1), jnp.int32(0)) if nq > 1 else (jnp.int32(0), jnp.int32(0))
    stage_scores(first, 0)
    stage_scores(following(first), 1)
    idle = (jnp.int32(0), jnp.int32(0))

    def make_body(diagonal):
        def body(_, carry):
            done2, done1, cur = carry
            nxt1 = following(cur)
            nxt2 = following(nxt1)
            nxt3 = following(nxt2)
            stage_values(done2, 0)
            stage_softmax(cur, 0, diagonal)
            stage_scores(nxt2, 0)
            stage_values(done1, 1)
            stage_softmax(nxt1, 1, diagonal)
            stage_scores(nxt3, 1)
            return cur, nxt1, nxt2
        return body

    carry = lax.fori_loop(0, n_off // 2, make_body(False), (idle, idle, first), unroll=7 if n_off % 14 == 0 else 1)
    done2, done1, _ = lax.fori_loop(0, nq // 2, make_body(True), carry, unroll=True)
    stage_values(done2, 0)
    stage_values(done1, 1)

    for i in range(nq):
        inv_l = [1.0 / acc_ref[i, mp, dv:dv + 1, :] for mp in range(2)]
        oT = acc_ref[i, 0, 0:dv, :] * inv_l[0] - lam * (acc_ref[i, 1, 0:dv, :] * inv_l[1])
        ms = jnp.mean(oT * oT, axis=0, keepdims=True)
        y = oT * lax.rsqrt(ms + SUBLN_EPS) * sw_col * (1.0 - LAMBDA_INIT)
        o_ref[i * tq:(i + 1) * tq, :] = y.T.astype(BF16)


def _diff_attention(qT, k, vT, lq1, lk1, lq2, lk2, subln_w, batch, seq, tq):
    t = batch * seq
    vec = lambda a: a.reshape(1, -1)
    rows_blk = pl.BlockSpec((seq, LANES), lambda b, h: (b, h))
    cols_blk = pl.BlockSpec((LANES, seq), lambda b, h: (h, b))
    return pl.pallas_call(
        functools.partial(_attn_kernel, seq=seq, tq=tq),
        grid=(batch, ATTN_N_HEADS),
        in_specs=[
            cols_blk, rows_blk, cols_blk,
            _const_spec((1, ATTN_HEAD_DIM)),
            _const_spec((1, ATTN_HEAD_DIM)),
            _const_spec((1, ATTN_HEAD_DIM)),
            _const_spec((1, ATTN_HEAD_DIM)),
            _const_spec((2 * ATTN_HEAD_DIM, 1)),
        ],
        out_specs=rows_blk,
        out_shape=jax.ShapeDtypeStruct((t, D_MODEL), BF16),
        scratch_shapes=[
            pltpu.VMEM((2, 2, tq, tq), F32),
            pltpu.VMEM((2, 2, tq, tq), BF16),
            pltpu.VMEM((2, 2, 1, tq), F32),
            pltpu.VMEM((seq // tq, 2, 1, tq), F32),
            pltpu.VMEM((seq // tq, 2, LANES + ONES_ROWS, tq), F32),
        ],
        compiler_params=pltpu.CompilerParams(
            dimension_semantics=("arbitrary", "arbitrary"),
            vmem_limit_bytes=VMEM_LIMIT_BYTES),
        name="diff_attention",
    )(qT, k, vT, vec(lq1), vec(lk1), vec(lq2), vec(lk2), subln_w.reshape(-1, 1))


def _merge_kernel(x_ref, ys_ref, ya_ref, gs_ref, ga_ref, wbs_ref, wba_ref, wo_ref, o_ref):
    a = jnp.dot(ys_ref[...], wbs_ref[...], preferred_element_type=F32)
    b = jnp.dot(ya_ref[...], wba_ref[...], preferred_element_type=F32)
    merged = (jax.nn.sigmoid(gs_ref[...].astype(F32)) * a
              + jax.nn.sigmoid(ga_ref[...].astype(F32)) * b)
    o_ref[...] = x_ref[...] + jnp.dot(merged.astype(BF16), wo_ref[...], preferred_element_type=F32)


def _merge(x2d, ys, ya, gs, ga, wbs, wba, wo, tm):
    t, d = x2d.shape
    row = lambda i: (i, 0)
    tile = pl.BlockSpec((tm, d), row)
    return pl.pallas_call(
        _merge_kernel,
        grid=(t // tm,),
        in_specs=[tile, tile, tile, tile, tile,
                  _const_spec((d, d)), _const_spec((d, d)), _const_spec((d, d))],
        out_specs=tile,
        out_shape=jax.ShapeDtypeStruct((t, d), F32),
        compiler_params=pltpu.CompilerParams(
            dimension_semantics=("arbitrary",), vmem_limit_bytes=VMEM_LIMIT_BYTES),
        name="gated_merge_out_projection",
    )(x2d, ys, ya, gs, ga, wbs, wba, wo)


def _ffn_kernel(x_ref, halo_ref, nw_ref, wup_ref, cw_ref, cb_ref, wd_ref, fw_ref, o_ref, gbuf_ref,
                *, tm, tiles_per_seq, chunk):
    i = pl.program_id(0)
    x1 = x_ref[...]
    x_ext = jnp.concatenate([halo_ref[...], x1], axis=0)
    h_ext = _rms(x_ext, nw_ref[...], NORM_EPS).astype(BF16)
    h2 = h_ext[FFN_HALO:, :]
    keep_halo = jnp.where(i % tiles_per_seq == 0, 0.0, 1.0)
    cw = cw_ref[...]
    cb = cb_ref[...]
    KW = FFN_CONV_WIDTH
    n_chunks = D_FF // chunk

    def up(c):
        lo = c * chunk
        gate_ext = jnp.dot(h_ext, wup_ref[:, lo:lo + chunk], preferred_element_type=F32)
        val = jnp.dot(h2, wup_ref[:, D_FF + lo:D_FF + lo + chunk], preferred_element_type=F32)
        return gate_ext, val

    acc = jnp.zeros((tm, D_MODEL), F32)
    nxt = up(0)
    for c in range(n_chunks):
        lo = c * chunk
        gate_ext, val = nxt
        if c + 1 < n_chunks:
            nxt = up(c + 1)
        slot = c % 2
        gbuf_ref[slot, 0:FFN_HALO, :] = gate_ext[0:FFN_HALO, :] * keep_halo
        gbuf_ref[slot, FFN_HALO:, :] = gate_ext[FFN_HALO:, :]
        conv = cb[:, lo:lo + chunk] + cw[KW - 1:KW, lo:lo + chunk] * gate_ext[FFN_HALO:, :]
        for j in range(KW - 1):
            shift = KW - 1 - j
            conv = conv + cw[j:j + 1, lo:lo + chunk] * gbuf_ref[slot, pl.ds(FFN_HALO - shift, tm), :]
        act = (_silu(conv) * val).astype(BF16)
        acc = acc + jnp.dot(act, wd_ref[lo:lo + chunk, :], preferred_element_type=F32)
    o_ref[...] = _rms(x1 + acc, fw_ref[...], NORM_EPS)


def _conv_ffn(x1, norm_w, w_up, conv_w, conv_b, w_down, final_w, seq, tm, chunk):
    t, d = x1.shape
    tiles_per_seq = seq // tm
    halo_blocks = tm // FFN_HALO
    return pl.pallas_call(
        functools.partial(_ffn_kernel, tm=tm, tiles_per_seq=tiles_per_seq, chunk=chunk),
        grid=(t // tm,),
        in_specs=[
            pl.BlockSpec((tm, d), lambda i: (i, 0)),
            pl.BlockSpec((FFN_HALO, d), lambda i: (jnp.maximum(i * halo_blocks - 1, 0), 0)),
            _const_spec((1, d)),
            _const_spec(w_up.shape),
            _const_spec(conv_w.shape),
            _const_spec((1, D_FF)),
            _const_spec(w_down.shape),
            _const_spec((1, d)),
        ],
        out_specs=pl.BlockSpec((tm, d), lambda i: (i, 0)),
        out_shape=jax.ShapeDtypeStruct((t, d), F32),
        scratch_shapes=[pltpu.VMEM((2, FFN_HALO + tm, chunk), F32)],
        compiler_params=pltpu.CompilerParams(
            dimension_semantics=("arbitrary",), vmem_limit_bytes=VMEM_LIMIT_BYTES),
        name="conv_ffn_final_norm",
    )(x1, x1, norm_w.reshape(1, -1), w_up, conv_w, conv_b.reshape(1, -1), w_down, final_w.reshape(1, -1))


def _rope_tables(seq):
    half = ATTN_HEAD_DIM // 2
    inv = 1.0 / (ROPE_THETA ** (jnp.arange(0, ATTN_HEAD_DIM, 2, dtype=F32) / ATTN_HEAD_DIM))
    ang = jnp.arange(seq, dtype=F32)[:, None] * inv[None, :]
    lane = jnp.arange(LANES)
    cos = jnp.cos(ang)[:, lane % half]
    sin = jnp.sin(ang)[:, lane % half]
    second = (lane % ATTN_HEAD_DIM) >= half
    sa = jnp.where(second[None, :], sin, 0.0)
    sb = jnp.where(second[None, :], 0.0, -sin)
    return cos, sa, sb, jnp.cos(ang).T, jnp.sin(ang).T


def _tile(n, pref):
    while n % pref:
        pref //= 2
    return pref


def kernel(x, norm_mix_w, w_in, ssd_conv_w, ssd_conv_b, ssd_dt_bias, ssd_a_log, ssd_d_skip, ssd_norm_w, lambda_q1, lambda_k1, lambda_q2, lambda_k2, subln_w, w_branch_ssd, w_branch_attn, w_out, norm_ffn_w, w_up, ffn_conv_w, ffn_conv_b, w_down, final_norm_w):
    batch, seq, d = x.shape
    assert d == D_MODEL and seq % SSD_CHUNK == 0
    t = batch * seq
    x2d = x.reshape(t, d)
    li = 0

    w = w_in[li]
    o_z, o_xbc = 0, D_MODEL
    o_dt = o_xbc + SSD_CONV_DIM
    o_q = o_dt + SSD_N_HEADS
    cols = lambda lo, n: w[:, lo:lo + n]
    w_dt = cols(o_dt, SSD_N_HEADS)
    w_all = jnp.concatenate([
        cols(o_z, D_MODEL), cols(o_q + D_MODEL, D_MODEL),
        cols(o_q + 3 * D_MODEL, D_MODEL), cols(o_q + 4 * D_MODEL, D_MODEL), cols(o_xbc, SSD_CONV_DIM),
        jnp.pad(w_dt, ((0, 0), (0, LANES - SSD_N_HEADS)))], axis=1).astype(BF16)
    assert w_all.shape[1] == _IN_COLS_PADDED
    w_dtT = w_dt.T.astype(BF16)
    w_qT = cols(o_q, D_MODEL).T.astype(BF16)
    w_vT = cols(o_q + 2 * D_MODEL, D_MODEL).T.astype(BF16)

    tm = _tile(seq, 512)
    z, qT, k, vT, gs, ga, xbc, dt, dtT = _in_projection(
        x2d, norm_mix_w[li].reshape(1, -1), w_all, w_qT, w_vT, w_dtT, _rope_tables(seq), seq, tm)

    y_ssd = _ssd_branch(z, xbc, dt, dtT, ssd_conv_w[li], ssd_conv_b[li], ssd_dt_bias[li], ssd_a_log[li],
                        ssd_d_skip[li], ssd_norm_w[li], batch, seq)
    tq = _tile(seq, 256)
    y_attn = _diff_attention(qT, k, vT, lambda_q1[li], lambda_k1[li], lambda_q2[li], lambda_k2[li],
                             subln_w[li], batch, seq, tq)
    x1 = _merge(x2d, y_ssd, y_attn, gs, ga, w_branch_ssd[li].astype(BF16), w_branch_attn[li].astype(BF16),
                w_out[li].astype(BF16), tm)
    out = _conv_ffn(x1, norm_ffn_w[li], w_up[li].astype(BF16), ffn_conv_w[li], ffn_conv_b[li],
                    w_down[li].astype(BF16), final_norm_w, seq, tm, 256)
    return out.reshape(batch, seq, d)
```

```python
import functools
import math

import jax
import jax.numpy as jnp
from jax import lax
from jax.experimental import pallas as pl
from jax.experimental.pallas import tpu as pltpu

F32 = jnp.float32
BF16 = jnp.bfloat16

D_MODEL = 1024
SSD_HEAD_DIM = 64
SSD_N_HEADS = 16
SSD_N_GROUPS = 2
SSD_D_STATE = 128
SSD_CONV_WIDTH = 4
SSD_CHUNK = 128
SSD_CONV_DIM = D_MODEL + 2 * SSD_N_GROUPS * SSD_D_STATE
ATTN_HEAD_DIM = 64
ATTN_N_HEADS = 8
ROPE_THETA = 10000.0
D_FF = 2816
FFN_CONV_WIDTH = 3
NORM_EPS = 1e-6
SUBLN_EPS = 1e-5
LAMBDA_INIT = 0.8 - 0.6 * math.exp(-0.3 * 0)

LANES = 128
SUBLANES = 8
ONES_ROWS = 16
ROW_HALO = 16
FFN_HALO = ROW_HALO
CONV_CHUNK = 256
DOWN_GROUP = 4
VMEM_LIMIT_BYTES = 56 * 1024 * 1024

_OFF_Z, _OFF_K, _OFF_GS, _OFF_GA = (i * D_MODEL for i in range(4))
_OFF_XBC = 4 * D_MODEL
_OFF_DT = _OFF_XBC + SSD_CONV_DIM
_IN_COLS_PADDED = _OFF_DT + LANES


def _const_spec(shape):
    nd = len(shape)
    return pl.BlockSpec(shape, lambda *_: (0,) * nd, pipeline_mode=pl.Buffered(1))


def _rms(x, w, eps):
    return x * lax.rsqrt(jnp.mean(x * x, axis=-1, keepdims=True) + eps) * w


def _split2(x):
    hi = x.astype(BF16)
    lo = (x - hi.astype(F32)).astype(BF16)
    return hi, lo


def _split3(x):
    hi = x.astype(BF16)
    r = x - hi.astype(F32)
    mid = r.astype(BF16)
    lo = (r - mid.astype(F32)).astype(BF16)
    return hi, mid, lo


def _softplus(x):
    return jnp.maximum(x, 0.0) + jnp.log1p(jnp.exp(-jnp.abs(x)))


def _silu(x):
    return x * jax.nn.sigmoid(x)


def _inproj_kernel(x_ref, halo_ref, nw_ref, w_ref, wqT_ref, wvT_ref, wdt_ref, cw_ref, cb_ref, dtb_row_ref,
                   dtb_col_ref, cos_ref, sa_ref, sb_ref, cosT_ref, sinT_ref,
                   zs_ref, qT_ref, k_ref, vT_ref, gs_ref, ga_ref, xc_ref, dt_ref, dtT_ref, cbuf_ref,
                   *, tm, tiles_per_seq):
    i = pl.program_id(0)
    x_ext = jnp.concatenate([halo_ref[...], x_ref[...]], axis=0)
    h_ext = _rms(x_ext, nw_ref[...], NORM_EPS).astype(BF16)
    h = h_ext[ROW_HALO:, :]
    keep_halo = jnp.where(i % tiles_per_seq == 0, 0.0, 1.0)
    nt = (((1,), (1,)), ((), ()))

    def mm(lo, width):
        return jnp.dot(h, w_ref[:, lo:lo + width], preferred_element_type=F32)

    half = ATTN_HEAD_DIM // 2

    def emit_zs():
        zs_ref[...] = _silu(mm(_OFF_Z, D_MODEL)).astype(BF16)

    def emit_gates_dt():
        gs_ref[...] = mm(_OFF_GS, D_MODEL).astype(BF16)
        dt_ref[...] = _softplus(mm(_OFF_DT, LANES) + dtb_row_ref[...])
        dtT_ref[...] = _softplus(lax.dot_general(wdt_ref[...], h, nt, preferred_element_type=F32)
                                 + dtb_col_ref[...])

    def emit_ga():
        ga_ref[...] = mm(_OFF_GA, D_MODEL).astype(BF16)

    def emit_vT():
        vT_ref[...] = lax.dot_general(wvT_ref[...], h, nt, preferred_element_type=F32).astype(BF16)

    def emit_k():
        cos, sa, sb = cos_ref[...], sa_ref[...], sb_ref[...]
        acc = mm(_OFF_K, D_MODEL)
        for j in range(D_MODEL // LANES):
            blk = acc[:, j * LANES:(j + 1) * LANES]
            r = blk * cos + pltpu.roll(blk, half, 1) * sa + pltpu.roll(blk, LANES - half, 1) * sb
            k_ref[:, j * LANES:(j + 1) * LANES] = r.astype(BF16)

    def emit_qT():
        scale = ATTN_HEAD_DIM ** -0.5 * math.log2(math.e)
        cosT, sinT = cosT_ref[...] * scale, sinT_ref[...] * scale
        accT = lax.dot_general(wqT_ref[...], h, nt, preferred_element_type=F32)
        for g in range(D_MODEL // ATTN_HEAD_DIM):
            lo = g * ATTN_HEAD_DIM
            x1 = accT[lo:lo + half, :]
            x2 = accT[lo + half:lo + 2 * half, :]
            qT_ref[lo:lo + half, :] = (x1 * cosT - x2 * sinT).astype(BF16)
            qT_ref[lo + half:lo + 2 * half, :] = (x2 * cosT + x1 * sinT).astype(BF16)

    cw = cw_ref[...]
    cb = cb_ref[...]
    KW = SSD_CONV_WIDTH
    n_chunks = SSD_CONV_DIM // CONV_CHUNK
    fillers = [emit_zs, emit_gates_dt, emit_ga, emit_vT, emit_k, emit_qT]
    assert len(fillers) == n_chunks

    def xbc_product(c):
        lo = _OFF_XBC + c * CONV_CHUNK
        return jnp.dot(h_ext, w_ref[:, lo:lo + CONV_CHUNK], preferred_element_type=F32)

    nxt = xbc_product(0)
    for c in range(n_chunks):
        cols = slice(c * CONV_CHUNK, (c + 1) * CONV_CHUNK)
        xe = nxt
        if c + 1 < n_chunks:
            nxt = xbc_product(c + 1)
        fillers[c]()
        slot = c % 2
        cbuf_ref[slot, 0:ROW_HALO, :] = xe[0:ROW_HALO, :] * keep_halo
        cbuf_ref[slot, ROW_HALO:, :] = xe[ROW_HALO:, :]
        conv = cb[:, cols] + cw[KW - 1:KW, cols] * xe[ROW_HALO:, :]
        for j in range(KW - 1):
            shift = KW - 1 - j
            conv = conv + cw[j:j + 1, cols] * cbuf_ref[slot, pl.ds(ROW_HALO - shift, tm), :]
        xc_ref[:, cols] = _silu(conv).astype(BF16)


def _in_projection(x2d, norm_w, w_all, w_qT, w_vT, w_dtT, conv_w, conv_b, dt_bias, rope, seq, tm):
    t, d = x2d.shape
    n_seq_tiles = seq // tm
    cos_t, sa_t, sb_t, cosT_t, sinT_t = rope
    half = ATTN_HEAD_DIM // 2
    halo_blocks = tm // ROW_HALO
    row = lambda i: (i, 0)
    colblk = lambda i: (0, i)
    tab = lambda i: (i % n_seq_tiles, 0)
    tabT = lambda i: (0, i % n_seq_tiles)
    big = jax.ShapeDtypeStruct((t, D_MODEL), BF16)
    bigT = jax.ShapeDtypeStruct((D_MODEL, t), BF16)
    dtb_row = jnp.pad(dt_bias, (0, LANES - SSD_N_HEADS)).reshape(1, LANES)
    dtb_col = dt_bias.reshape(SSD_N_HEADS, 1)
    return pl.pallas_call(
        functools.partial(_inproj_kernel, tm=tm, tiles_per_seq=n_seq_tiles),
        grid=(t // tm,),
        in_specs=[
            pl.BlockSpec((tm, d), row),
            pl.BlockSpec((ROW_HALO, d), lambda i: (jnp.maximum(i * halo_blocks - 1, 0), 0)),
            _const_spec((1, d)),
            _const_spec(w_all.shape),
            _const_spec(w_qT.shape),
            _const_spec(w_vT.shape),
            _const_spec(w_dtT.shape),
            _const_spec((SSD_CONV_WIDTH, SSD_CONV_DIM)),
            _const_spec((1, SSD_CONV_DIM)),
            _const_spec((1, LANES)),
            _const_spec((SSD_N_HEADS, 1)),
            pl.BlockSpec((tm, LANES), tab),
            pl.BlockSpec((tm, LANES), tab),
            pl.BlockSpec((tm, LANES), tab),
            pl.BlockSpec((half, tm), tabT),
            pl.BlockSpec((half, tm), tabT),
        ],
        out_specs=[
            pl.BlockSpec((tm, D_MODEL), row),
            pl.BlockSpec((D_MODEL, tm), colblk),
            pl.BlockSpec((tm, D_MODEL), row),
            pl.BlockSpec((D_MODEL, tm), colblk),
            pl.BlockSpec((tm, D_MODEL), row),
            pl.BlockSpec((tm, D_MODEL), row),
            pl.BlockSpec((tm, SSD_CONV_DIM), row),
            pl.BlockSpec((tm, LANES), row),
            pl.BlockSpec((SSD_N_HEADS, tm), colblk),
        ],
        out_shape=[big, bigT, big, bigT, big, big,
                   jax.ShapeDtypeStruct((t, SSD_CONV_DIM), BF16),
                   jax.ShapeDtypeStruct((t, LANES), F32),
                   jax.ShapeDtypeStruct((SSD_N_HEADS, t), F32)],
        scratch_shapes=[pltpu.VMEM((2, ROW_HALO + tm, CONV_CHUNK), F32)],
        compiler_params=pltpu.CompilerParams(
            dimension_semantics=("arbitrary",), vmem_limit_bytes=VMEM_LIMIT_BYTES),
        name="norm_in_projection",
    )(x2d, x2d, norm_w, w_all, w_qT, w_vT, w_dtT, conv_w, conv_b.reshape(1, -1), dtb_row, dtb_col,
      cos_t, sa_t, sb_t, cosT_t, sinT_t)


def _ssd_kernel(zs_ref, xc_ref, dt_ref, dtT_ref, alog_row_ref, alog_col_ref, dexp_ref, nw_ref, expand_ref,
                y_ref, state_ref, *, chunks_per_step):
    L = SSD_CHUNK
    P = SSD_HEAD_DIM
    N = SSD_D_STATE
    heads_per_group = SSD_N_HEADS // SSD_N_GROUPS

    @pl.when(pl.program_id(1) == 0)
    def _():
        state_ref[...] = jnp.zeros_like(state_ref)

    neg_a_row = -jnp.exp(alog_row_ref[...])
    neg_a_col = -jnp.exp(alog_col_ref[...])
    ri = lax.broadcasted_iota(jnp.int32, (L, L), 0)
    ci = lax.broadcasted_iota(jnp.int32, (L, L), 1)
    causal = ri >= ci
    tri = jnp.where(causal, 1.0, 0.0).astype(BF16)
    triT = jnp.where(ri <= ci, 1.0, 0.0).astype(BF16)
    lane = lax.broadcasted_iota(jnp.int32, (L, LANES), 1)
    first_half = lane < P
    gw = D_MODEL // SSD_N_GROUPS

    def one_chunk(step, carry):
        rows = pl.ds(pl.multiple_of(step * L, L), L)
        xs_b = xc_ref[rows, :D_MODEL]
        xs = xs_b.astype(F32)

        dt = dt_ref[rows, :]
        dtT = dtT_ref[:, rows]
        dA = dt * neg_a_row
        dAT = dtT * neg_a_col
        a_cs = sum(jnp.dot(tri, part, preferred_element_type=F32) for part in _split3(dA))
        a_csT = sum(jnp.dot(part, triT, preferred_element_type=F32) for part in _split3(dAT))
        a_last = a_cs[L - 1:L, :]
        exp_a = jnp.exp(a_cs)
        w_state = dt * jnp.exp(a_last - a_cs)

        expand = expand_ref[...]
        w_state_x = sum(jnp.dot(part, expand, preferred_element_type=F32) for part in _split2(w_state))
        a_tail = jnp.broadcast_to(a_last, (SUBLANES, LANES))
        a_last_x = sum(jnp.dot(part, expand, preferred_element_type=F32) for part in _split3(a_tail))[0:1, :]
        xd_b = (xs * w_state_x).astype(BF16)

        y_parts = []
        new_state_parts = []
        for g in range(SSD_N_GROUPS):
            b_gb = xc_ref[rows, D_MODEL + g * N:D_MODEL + (g + 1) * N]
            c_gb = xc_ref[rows, D_MODEL + SSD_N_GROUPS * N + g * N:D_MODEL + SSD_N_GROUPS * N + (g + 1) * N]
            c_g = c_gb.astype(F32)
            cb = lax.dot_general(c_gb, b_gb, (((1,), (1,)), ((), ())), preferred_element_type=F32)
            for pair in range(heads_per_group // 2):
                col0 = (g * heads_per_group + 2 * pair) * P
                x_pair = xs_b[:, col0:col0 + 2 * P]
                prev_pair = state_ref[:, col0:col0 + 2 * P].astype(BF16)
                outs = []
                for k in range(2):
                    hd = g * heads_per_group + 2 * pair + k
                    seg = a_cs[:, hd:hd + 1] - a_csT[hd:hd + 1, :]
                    decay = jnp.exp(jnp.where(causal, seg, -jnp.inf))
                    m_h = (cb * decay * dtT[hd:hd + 1, :]).astype(BF16)
                    c_h = (c_g * exp_a[:, hd:hd + 1]).astype(BF16)
                    outs.append(jnp.dot(m_h, x_pair, preferred_element_type=F32)
                                + jnp.dot(c_h, prev_pair, preferred_element_type=F32))
                y_parts.append(jnp.where(first_half, outs[0], outs[1]))
            lo = g * heads_per_group * P
            hi = (g + 1) * heads_per_group * P
            new_state_parts.append(lax.dot_general(b_gb, xd_b[:, lo:hi], (((0,), (0,)), ((), ())),
                                                   preferred_element_type=F32))
        state_ref[...] = state_ref[...] * jnp.exp(a_last_x) + jnp.concatenate(new_state_parts, axis=1)

        y = jnp.concatenate(y_parts, axis=1) + xs * dexp_ref[...]
        yg = y * zs_ref[rows, :].astype(F32)
        nw = nw_ref[...]
        for g in range(SSD_N_GROUPS):
            part = yg[:, g * gw:(g + 1) * gw]
            y_ref[rows, g * gw:(g + 1) * gw] = _rms(part, nw[:, g * gw:(g + 1) * gw], NORM_EPS).astype(BF16)
        return carry

    lax.fori_loop(0, chunks_per_step, one_chunk, 0)


def _ssd_branch(zs, xc, dt, dtT, a_log, d_skip, norm_w, batch, seq):
    L = SSD_CHUNK
    nc = seq // L
    t = batch * seq
    alog_row = jnp.pad(a_log, (0, LANES - SSD_N_HEADS)).reshape(1, LANES)
    alog_col = a_log.reshape(SSD_N_HEADS, 1)
    dexp = jnp.repeat(d_skip, SSD_HEAD_DIM).reshape(1, D_MODEL)
    expand = (jnp.arange(LANES)[:, None] == (jnp.arange(D_MODEL)[None, :] // SSD_HEAD_DIM)).astype(BF16)
    cps = _tile(nc, 4)
    ns = nc // cps
    rows = cps * L
    row = lambda b, c: (b * ns + c, 0)
    return pl.pallas_call(
        functools.partial(_ssd_kernel, chunks_per_step=cps),
        grid=(batch, ns),
        in_specs=[
            pl.BlockSpec((rows, D_MODEL), row),
            pl.BlockSpec((rows, SSD_CONV_DIM), row),
            pl.BlockSpec((rows, LANES), row),
            pl.BlockSpec((SSD_N_HEADS, rows), lambda b, c: (0, b * ns + c)),
            _const_spec((1, LANES)),
            _const_spec((SSD_N_HEADS, 1)),
            _const_spec((1, D_MODEL)),
            _const_spec((1, D_MODEL)),
            _const_spec((LANES, D_MODEL)),
        ],
        out_specs=pl.BlockSpec((rows, D_MODEL), row),
        out_shape=jax.ShapeDtypeStruct((t, D_MODEL), BF16),
        scratch_shapes=[pltpu.VMEM((SSD_D_STATE, D_MODEL), F32)],
        compiler_params=pltpu.CompilerParams(
            dimension_semantics=("arbitrary", "arbitrary"), vmem_limit_bytes=VMEM_LIMIT_BYTES),
        name="ssd_branch",
    )(zs, xc, dt, dtT, alog_row, alog_col, dexp, norm_w.reshape(1, -1), expand)


def _attn_kernel(qT_ref, k_ref, vT_ref, lq1_ref, lk1_ref, lq2_ref, lk2_ref, swc_ref, o_ref,
                 s_ref, p_ref, alpha_ref, m_ref, acc_ref, *, seq, tq):
    d = ATTN_HEAD_DIM
    dv = 2 * d
    tk = tq
    nq = seq // tq
    n_off = nq * (nq - 1) // 2
    assert n_off % 2 == 0 and nq % 2 == 0, "step pairs need even counts"
    neg_inf = -jnp.inf

    p_ref[...] = jnp.zeros(p_ref.shape, BF16)
    alpha_ref[...] = jnp.ones(alpha_ref.shape, F32)
    m_ref[...] = jnp.full(m_ref.shape, neg_inf, F32)
    acc_ref[...] = jnp.zeros(acc_ref.shape, F32)
    key_i = lax.broadcasted_iota(jnp.int32, (tk, LANES), 0)
    qry_i = lax.broadcasted_iota(jnp.int32, (tk, LANES), 1)
    ones_rows = jnp.ones((ONES_ROWS, tk), BF16)
    feat = lax.broadcasted_iota(jnp.int32, (2 * d, tq), 0)
    map_rows = (feat < d, feat >= d)
    lam = (jnp.exp(jnp.sum(lq1_ref[...] * lk1_ref[...], axis=-1, keepdims=True))
           - jnp.exp(jnp.sum(lq2_ref[...] * lk2_ref[...], axis=-1, keepdims=True)) + LAMBDA_INIT)
    sw_col = swc_ref[...]

    def stage_scores(step, slot):
        i, j = step
        qT = qT_ref[:, pl.ds(pl.multiple_of(i * tq, tq), tq)]
        kb = k_ref[pl.ds(pl.multiple_of(j * tk, tk), tk), :]
        zero = jnp.zeros_like(qT)
        for mp in range(2):
            s_ref[slot, mp] = jnp.dot(kb, jnp.where(map_rows[mp], qT, zero), preferred_element_type=F32)

    def stage_softmax(step, slot, diagonal):
        i, _ = step
        for mp in range(2):
            for c in range(tq // LANES):
                cs = slice(c * LANES, (c + 1) * LANES)
                ss = s_ref[slot, mp, :, cs]
                if diagonal:
                    ss = jnp.where(key_i <= qry_i + c * LANES, ss, neg_inf)
                m_prev = m_ref[i, mp, :, cs]
                m_new = jnp.maximum(m_prev, jnp.max(ss, axis=0, keepdims=True))
                p_ref[slot, mp, :, cs] = jnp.exp2(ss - m_new).astype(BF16)
                alpha_ref[slot, mp, :, cs] = jnp.exp2(m_prev - m_new)
                m_ref[i, mp, :, cs] = m_new

    def stage_values(step, slot):
        i, j = step
        vT = vT_ref[:, pl.ds(pl.multiple_of(j * tk, tk), tk)]
        v_ext = jnp.concatenate([vT, ones_rows], axis=0)
        for mp in range(2):
            acc_ref[i, mp] = (alpha_ref[slot, mp] * acc_ref[i, mp]
                              + jnp.dot(v_ext, p_ref[slot, mp], preferred_element_type=F32))

    def following(step):
        i, j = step
        on_diag = j == i
        same_row = j + 1 < i
        off_i = jnp.where(same_row, i, i + 1)
        off_j = jnp.where(same_row, j + 1, 0)
        to_diag = off_i >= nq
        off_i = jnp.where(to_diag, 0, off_i)
        off_j = jnp.where(to_diag, 0, off_j)
        diag_i = jnp.minimum(i + 1, nq - 1)
        return jnp.where(on_diag, diag_i, off_i), jnp.where(on_diag, diag_i, off_j)

    first = (jnp.int32(1), jnp.int32(0)) if nq > 1 else (jnp.int32(0), jnp.int32(0))
    stage_scores(first, 0)
    stage_scores(following(first), 1)
    idle = (jnp.int32(0), jnp.int32(0))

    def make_body(diagonal):
        def body(_, carry):
            done2, done1, cur = carry
            nxt1 = following(cur)
            nxt2 = following(nxt1)
            nxt3 = following(nxt2)
            stage_values(done2, 0)
            stage_softmax(cur, 0, diagonal)
            stage_scores(nxt2, 0)
            stage_values(done1, 1)
            stage_softmax(nxt1, 1, diagonal)
            stage_scores(nxt3, 1)
            return cur, nxt1, nxt2
        return body

    carry = lax.fori_loop(0, n_off // 2, make_body(False), (idle, idle, first), unroll=2 if n_off % 4 == 0 else 1)
    done2, done1, _ = lax.fori_loop(0, nq // 2, make_body(True), carry, unroll=2 if nq % 4 == 0 else 1)
    stage_values(done2, 0)
    stage_values(done1, 1)

    for i in range(nq):
        inv_l = [1.0 / acc_ref[i, mp, dv:dv + 1, :] for mp in range(2)]
        oT = acc_ref[i, 0, 0:dv, :] * inv_l[0] - lam * (acc_ref[i, 1, 0:dv, :] * inv_l[1])
        ms = jnp.mean(oT * oT, axis=0, keepdims=True)
        y = oT * lax.rsqrt(ms + SUBLN_EPS) * sw_col * (1.0 - LAMBDA_INIT)
        o_ref[i * tq:(i + 1) * tq, :] = y.T.astype(BF16)


def _diff_attention(qT, k, vT, lq1, lk1, lq2, lk2, subln_w, batch, seq, tq):
    t = batch * seq
    vec = lambda a: a.reshape(1, -1)
    rows_blk = pl.BlockSpec((seq, LANES), lambda b, h: (b, h))
    cols_blk = pl.BlockSpec((LANES, seq), lambda b, h: (h, b))
    return pl.pallas_call(
        functools.partial(_attn_kernel, seq=seq, tq=tq),
        grid=(batch, ATTN_N_HEADS),
        in_specs=[
            cols_blk, rows_blk, cols_blk,
            _const_spec((1, ATTN_HEAD_DIM)),
            _const_spec((1, ATTN_HEAD_DIM)),
            _const_spec((1, ATTN_HEAD_DIM)),
            _const_spec((1, ATTN_HEAD_DIM)),
            _const_spec((2 * ATTN_HEAD_DIM, 1)),
        ],
        out_specs=rows_blk,
        out_shape=jax.ShapeDtypeStruct((t, D_MODEL), BF16),
        scratch_shapes=[
            pltpu.VMEM((2, 2, tq, tq), F32),
            pltpu.VMEM((2, 2, tq, tq), BF16),
            pltpu.VMEM((2, 2, 1, tq), F32),
            pltpu.VMEM((seq // tq, 2, 1, tq), F32),
            pltpu.VMEM((seq // tq, 2, LANES + ONES_ROWS, tq), F32),
        ],
        compiler_params=pltpu.CompilerParams(
            dimension_semantics=("arbitrary", "arbitrary"),
            vmem_limit_bytes=VMEM_LIMIT_BYTES),
        name="diff_attention",
    )(qT, k, vT, vec(lq1), vec(lk1), vec(lq2), vec(lk2), subln_w.reshape(-1, 1))


def _merge_kernel(x_ref, ys_ref, ya_ref, gs_ref, ga_ref, wbs_ref, wba_ref, wo_ref, o_ref):
    a = jnp.dot(ys_ref[...], wbs_ref[...], preferred_element_type=F32)
    b = jnp.dot(ya_ref[...], wba_ref[...], preferred_element_type=F32)
    merged = (jax.nn.sigmoid(gs_ref[...].astype(F32)) * a
              + jax.nn.sigmoid(ga_ref[...].astype(F32)) * b)
    o_ref[...] = x_ref[...] + jnp.dot(merged.astype(BF16), wo_ref[...], preferred_element_type=F32)


def _merge(x2d, ys, ya, gs, ga, wbs, wba, wo, tm):
    t, d = x2d.shape
    row = lambda i: (i, 0)
    tile = pl.BlockSpec((tm, d), row)
    return pl.pallas_call(
        _merge_kernel,
        grid=(t // tm,),
        in_specs=[tile, tile, tile, tile, tile,
                  _const_spec((d, d)), _const_spec((d, d)), _const_spec((d, d))],
        out_specs=tile,
        out_shape=jax.ShapeDtypeStruct((t, d), F32),
        compiler_params=pltpu.CompilerParams(
            dimension_semantics=("arbitrary",), vmem_limit_bytes=VMEM_LIMIT_BYTES),
        name="gated_merge_out_projection",
    )(x2d, ys, ya, gs, ga, wbs, wba, wo)


def _ffn_kernel(x_ref, halo_ref, nw_ref, wup_ref, cw_ref, cb_ref, wd_ref, fw_ref, o_ref, gbuf_ref, act_ref,
                *, tm, tiles_per_seq, chunk):
    i = pl.program_id(0)
    x1 = x_ref[...]
    x_ext = jnp.concatenate([halo_ref[...], x1], axis=0)
    h_ext = _rms(x_ext, nw_ref[...], NORM_EPS).astype(BF16)
    h2 = h_ext[FFN_HALO:, :]
    keep_halo = jnp.where(i % tiles_per_seq == 0, 0.0, 1.0)
    cw = cw_ref[...]
    cb = cb_ref[...]
    KW = FFN_CONV_WIDTH
    n_chunks = D_FF // chunk

    def up(c):
        lo = c * chunk
        gate_ext = jnp.dot(h_ext, wup_ref[:, lo:lo + chunk], preferred_element_type=F32)
        val = jnp.dot(h2, wup_ref[:, D_FF + lo:D_FF + lo + chunk], preferred_element_type=F32)
        return gate_ext, val

    acc = jnp.zeros((tm, D_MODEL), F32)
    nxt = up(0)
    for c in range(n_chunks):
        lo = c * chunk
        gate_ext, val = nxt
        if c + 1 < n_chunks:
            nxt = up(c + 1)
        slot = c % 2
        gbuf_ref[slot, 0:FFN_HALO, :] = gate_ext[0:FFN_HALO, :] * keep_halo
        gbuf_ref[slot, FFN_HALO:, :] = gate_ext[FFN_HALO:, :]
        conv = cb[:, lo:lo + chunk] + cw[KW - 1:KW, lo:lo + chunk] * gate_ext[FFN_HALO:, :]
        for j in range(KW - 1):
            shift = KW - 1 - j
            conv = conv + cw[j:j + 1, lo:lo + chunk] * gbuf_ref[slot, pl.ds(FFN_HALO - shift, tm), :]
        act_ref[:, lo:lo + chunk] = (_silu(conv) * val).astype(BF16)
        if (c + 1) % DOWN_GROUP == 0 or c + 1 == n_chunks:
            k0 = (c // DOWN_GROUP) * DOWN_GROUP * chunk
            acc = acc + jnp.dot(act_ref[:, k0:lo + chunk], wd_ref[k0:lo + chunk, :], preferred_element_type=F32)
    o_ref[...] = _rms(x1 + acc, fw_ref[...], NORM_EPS)


def _conv_ffn(x1, norm_w, w_up, conv_w, conv_b, w_down, final_w, seq, tm, chunk):
    t, d = x1.shape
    tiles_per_seq = seq // tm
    halo_blocks = tm // FFN_HALO
    return pl.pallas_call(
        functools.partial(_ffn_kernel, tm=tm, tiles_per_seq=tiles_per_seq, chunk=chunk),
        grid=(t // tm,),
        in_specs=[
            pl.BlockSpec((tm, d), lambda i: (i, 0)),
            pl.BlockSpec((FFN_HALO, d), lambda i: (jnp.maximum(i * halo_blocks - 1, 0), 0)),
            _const_spec((1, d)),
            _const_spec(w_up.shape),
            _const_spec(conv_w.shape),
            _const_spec((1, D_FF)),
            _const_spec(w_down.shape),
            _const_spec((1, d)),
        ],
        out_specs=pl.BlockSpec((tm, d), lambda i: (i, 0)),
        out_shape=jax.ShapeDtypeStruct((t, d), F32),
        scratch_shapes=[pltpu.VMEM((2, FFN_HALO + tm, chunk), F32), pltpu.VMEM((tm, D_FF), BF16)],
        compiler_params=pltpu.CompilerParams(
            dimension_semantics=("arbitrary",), vmem_limit_bytes=VMEM_LIMIT_BYTES),
        name="conv_ffn_final_norm",
    )(x1, x1, norm_w.reshape(1, -1), w_up, conv_w, conv_b.reshape(1, -1), w_down, final_w.reshape(1, -1))


def _rope_tables(seq):
    half = ATTN_HEAD_DIM // 2
    inv = 1.0 / (ROPE_THETA ** (jnp.arange(0, ATTN_HEAD_DIM, 2, dtype=F32) / ATTN_HEAD_DIM))
    ang = jnp.arange(seq, dtype=F32)[:, None] * inv[None, :]
    lane = jnp.arange(LANES)
    cos = jnp.cos(ang)[:, lane % half]
    sin = jnp.sin(ang)[:, lane % half]
    second = (lane % ATTN_HEAD_DIM) >= half
    sa = jnp.where(second[None, :], sin, 0.0)
    sb = jnp.where(second[None, :], 0.0, -sin)
    return cos, sa, sb, jnp.cos(ang).T, jnp.sin(ang).T


def _tile(n, pref):
    while n % pref:
        pref //= 2
    return pref


def kernel(x, norm_mix_w, w_in, ssd_conv_w, ssd_conv_b, ssd_dt_bias, ssd_a_log, ssd_d_skip, ssd_norm_w, lambda_q1, lambda_k1, lambda_q2, lambda_k2, subln_w, w_branch_ssd, w_branch_attn, w_out, norm_ffn_w, w_up, ffn_conv_w, ffn_conv_b, w_down, final_norm_w):
    batch, seq, d = x.shape
    assert d == D_MODEL and seq % SSD_CHUNK == 0
    t = batch * seq
    x2d = x.reshape(t, d)
    li = 0

    w = w_in[li]
    o_z, o_xbc = 0, D_MODEL
    o_dt = o_xbc + SSD_CONV_DIM
    o_q = o_dt + SSD_N_HEADS
    cols = lambda lo, n: w[:, lo:lo + n]
    w_dt = cols(o_dt, SSD_N_HEADS)
    w_all = jnp.concatenate([
        cols(o_z, D_MODEL), cols(o_q + D_MODEL, D_MODEL),
        cols(o_q + 3 * D_MODEL, D_MODEL), cols(o_q + 4 * D_MODEL, D_MODEL), cols(o_xbc, SSD_CONV_DIM),
        jnp.pad(w_dt, ((0, 0), (0, LANES - SSD_N_HEADS)))], axis=1).astype(BF16)
    assert w_all.shape[1] == _IN_COLS_PADDED
    w_dtT = w_dt.T.astype(BF16)
    w_qT = cols(o_q, D_MODEL).T.astype(BF16)
    w_vT = cols(o_q + 2 * D_MODEL, D_MODEL).T.astype(BF16)

    tm = _tile(seq, 512)
    zs, qT, k, vT, gs, ga, xc, dt, dtT = _in_projection(
        x2d, norm_mix_w[li].reshape(1, -1), w_all, w_qT, w_vT, w_dtT, ssd_conv_w[li], ssd_conv_b[li],
        ssd_dt_bias[li], _rope_tables(seq), seq, tm)

    y_ssd = _ssd_branch(zs, xc, dt, dtT, ssd_a_log[li], ssd_d_skip[li], ssd_norm_w[li], batch, seq)
    tq = _tile(seq, 256)
    y_attn = _diff_attention(qT, k, vT, lambda_q1[li], lambda_k1[li], lambda_q2[li], lambda_k2[li],
                             subln_w[li], batch, seq, tq)
    x1 = _merge(x2d, y_ssd, y_attn, gs, ga, w_branch_ssd[li].astype(BF16), w_branch_attn[li].astype(BF16),
                w_out[li].astype(BF16), tm)
    out = _conv_ffn(x1, norm_ffn_w[li], w_up[li].astype(BF16), ffn_conv_w[li], ffn_conv_b[li],
                    w_down[li].astype(BF16), final_norm_w, seq, tm, 256)
    return out.reshape(batch, seq, d)
```

```python
import functools
import math

import jax
import jax.numpy as jnp
from jax import lax
from jax.experimental import pallas as pl
from jax.experimental.pallas import tpu as pltpu

F32 = jnp.float32
BF16 = jnp.bfloat16

D_MODEL = 1024
SSD_HEAD_DIM = 64
SSD_N_HEADS = 16
SSD_N_GROUPS = 2
SSD_D_STATE = 128
SSD_CONV_WIDTH = 4
SSD_CHUNK = 128
SSD_CONV_DIM = D_MODEL + 2 * SSD_N_GROUPS * SSD_D_STATE
ATTN_HEAD_DIM = 64
ATTN_N_HEADS = 8
ROPE_THETA = 10000.0
D_FF = 2816
FFN_CONV_WIDTH = 3
NORM_EPS = 1e-6
SUBLN_EPS = 1e-5
LAMBDA_INIT = 0.8 - 0.6 * math.exp(-0.3 * 0)

LANES = 128
SUBLANES = 8
ONES_ROWS = 16
ROW_HALO = 16
FFN_HALO = ROW_HALO
CONV_CHUNK = 256
DOWN_GROUP = 4
VMEM_LIMIT_BYTES = 56 * 1024 * 1024

_OFF_Z, _OFF_K, _OFF_GS, _OFF_GA = (i * D_MODEL for i in range(4))
_OFF_XBC = 4 * D_MODEL
_OFF_DT = _OFF_XBC + SSD_CONV_DIM
_IN_COLS_PADDED = _OFF_DT + LANES


def _const_spec(shape):
    nd = len(shape)
    return pl.BlockSpec(shape, lambda *_: (0,) * nd, pipeline_mode=pl.Buffered(1))


def _rms(x, w, eps):
    return x * lax.rsqrt(jnp.mean(x * x, axis=-1, keepdims=True) + eps) * w


def _split2(x):
    hi = x.astype(BF16)
    lo = (x - hi.astype(F32)).astype(BF16)
    return hi, lo


def _split3(x):
    hi = x.astype(BF16)
    r = x - hi.astype(F32)
    mid = r.astype(BF16)
    lo = (r - mid.astype(F32)).astype(BF16)
    return hi, mid, lo


def _softplus(x):
    return jnp.maximum(x, 0.0) + jnp.log1p(jnp.exp(-jnp.abs(x)))


def _silu(x):
    return x * jax.nn.sigmoid(x)


def _inproj_kernel(x_ref, halo_ref, nw_ref, w_ref, wqT_ref, wvT_ref, wdt_ref, cw_ref, cb_ref, dtb_row_ref,
                   dtb_col_ref, cos_ref, sa_ref, sb_ref, cosT_ref, sinT_ref,
                   zs_ref, qT_ref, k_ref, vT_ref, gs_ref, ga_ref, xc_ref, dt_ref, dtT_ref, cbuf_ref,
                   *, tm, tiles_per_seq):
    i = pl.program_id(0)
    x_ext = jnp.concatenate([halo_ref[...], x_ref[...]], axis=0)
    h_ext = _rms(x_ext, nw_ref[...], NORM_EPS).astype(BF16)
    h = h_ext[ROW_HALO:, :]
    keep_halo = jnp.where(i % tiles_per_seq == 0, 0.0, 1.0)
    nt = (((1,), (1,)), ((), ()))

    def mm(lo, width):
        return jnp.dot(h, w_ref[:, lo:lo + width], preferred_element_type=F32)

    half = ATTN_HEAD_DIM // 2

    def emit_zs():
        zs_ref[...] = _silu(mm(_OFF_Z, D_MODEL)).astype(BF16)

    def emit_gates_dt():
        gs_ref[...] = mm(_OFF_GS, D_MODEL).astype(BF16)
        dt_ref[...] = _softplus(mm(_OFF_DT, LANES) + dtb_row_ref[...])
        dtT_ref[...] = _softplus(lax.dot_general(wdt_ref[...], h, nt, preferred_element_type=F32)
                                 + dtb_col_ref[...])

    def emit_ga():
        ga_ref[...] = mm(_OFF_GA, D_MODEL).astype(BF16)

    def emit_vT():
        vT_ref[...] = lax.dot_general(wvT_ref[...], h, nt, preferred_element_type=F32).astype(BF16)

    def emit_k():
        cos, sa, sb = cos_ref[...], sa_ref[...], sb_ref[...]
        acc = mm(_OFF_K, D_MODEL)
        for j in range(D_MODEL // LANES):
            blk = acc[:, j * LANES:(j + 1) * LANES]
            r = blk * cos + pltpu.roll(blk, half, 1) * sa + pltpu.roll(blk, LANES - half, 1) * sb
            k_ref[:, j * LANES:(j + 1) * LANES] = r.astype(BF16)

    def emit_qT():
        scale = ATTN_HEAD_DIM ** -0.5 * math.log2(math.e)
        cosT, sinT = cosT_ref[...] * scale, sinT_ref[...] * scale
        accT = lax.dot_general(wqT_ref[...], h, nt, preferred_element_type=F32)
        for g in range(D_MODEL // ATTN_HEAD_DIM):
            lo = g * ATTN_HEAD_DIM
            x1 = accT[lo:lo + half, :]
            x2 = accT[lo + half:lo + 2 * half, :]
            qT_ref[lo:lo + half, :] = (x1 * cosT - x2 * sinT).astype(BF16)
            qT_ref[lo + half:lo + 2 * half, :] = (x2 * cosT + x1 * sinT).astype(BF16)

    cw = cw_ref[...]
    cb = cb_ref[...]
    KW = SSD_CONV_WIDTH
    n_chunks = SSD_CONV_DIM // CONV_CHUNK
    fillers = [emit_zs, emit_gates_dt, emit_ga, emit_vT, emit_k, emit_qT]
    assert len(fillers) == n_chunks

    def xbc_product(c):
        lo = _OFF_XBC + c * CONV_CHUNK
        return jnp.dot(h_ext, w_ref[:, lo:lo + CONV_CHUNK], preferred_element_type=F32)

    nxt = xbc_product(0)
    for c in range(n_chunks):
        cols = slice(c * CONV_CHUNK, (c + 1) * CONV_CHUNK)
        xe = nxt
        if c + 1 < n_chunks:
            nxt = xbc_product(c + 1)
        fillers[c]()
        slot = c % 2
        cbuf_ref[slot, 0:ROW_HALO, :] = xe[0:ROW_HALO, :] * keep_halo
        cbuf_ref[slot, ROW_HALO:, :] = xe[ROW_HALO:, :]
        conv = cb[:, cols] + cw[KW - 1:KW, cols] * xe[ROW_HALO:, :]
        for j in range(KW - 1):
            shift = KW - 1 - j
            conv = conv + cw[j:j + 1, cols] * cbuf_ref[slot, pl.ds(ROW_HALO - shift, tm), :]
        xc_ref[:, cols] = _silu(conv).astype(BF16)


def _in_projection(x2d, norm_w, w_all, w_qT, w_vT, w_dtT, conv_w, conv_b, dt_bias, rope, seq, tm):
    t, d = x2d.shape
    n_seq_tiles = seq // tm
    cos_t, sa_t, sb_t, cosT_t, sinT_t = rope
    half = ATTN_HEAD_DIM // 2
    halo_blocks = tm // ROW_HALO
    row = lambda i: (i, 0)
    colblk = lambda i: (0, i)
    tab = lambda i: (i % n_seq_tiles, 0)
    tabT = lambda i: (0, i % n_seq_tiles)
    big = jax.ShapeDtypeStruct((t, D_MODEL), BF16)
    bigT = jax.ShapeDtypeStruct((D_MODEL, t), BF16)
    dtb_row = jnp.pad(dt_bias, (0, LANES - SSD_N_HEADS)).reshape(1, LANES)
    dtb_col = dt_bias.reshape(SSD_N_HEADS, 1)
    return pl.pallas_call(
        functools.partial(_inproj_kernel, tm=tm, tiles_per_seq=n_seq_tiles),
        grid=(t // tm,),
        in_specs=[
            pl.BlockSpec((tm, d), row),
            pl.BlockSpec((ROW_HALO, d), lambda i: (jnp.maximum(i * halo_blocks - 1, 0), 0)),
            _const_spec((1, d)),
            _const_spec(w_all.shape),
            _const_spec(w_qT.shape),
            _const_spec(w_vT.shape),
            _const_spec(w_dtT.shape),
            _const_spec((SSD_CONV_WIDTH, SSD_CONV_DIM)),
            _const_spec((1, SSD_CONV_DIM)),
            _const_spec((1, LANES)),
            _const_spec((SSD_N_HEADS, 1)),
            pl.BlockSpec((tm, LANES), tab),
            pl.BlockSpec((tm, LANES), tab),
            pl.BlockSpec((tm, LANES), tab),
            pl.BlockSpec((half, tm), tabT),
            pl.BlockSpec((half, tm), tabT),
        ],
        out_specs=[
            pl.BlockSpec((tm, D_MODEL), row),
            pl.BlockSpec((D_MODEL, tm), colblk),
            pl.BlockSpec((tm, D_MODEL), row),
            pl.BlockSpec((D_MODEL, tm), colblk),
            pl.BlockSpec((tm, D_MODEL), row),
            pl.BlockSpec((tm, D_MODEL), row),
            pl.BlockSpec((tm, SSD_CONV_DIM), row),
            pl.BlockSpec((tm, LANES), row),
            pl.BlockSpec((SSD_N_HEADS, tm), colblk),
        ],
        out_shape=[big, bigT, big, bigT, big, big,
                   jax.ShapeDtypeStruct((t, SSD_CONV_DIM), BF16),
                   jax.ShapeDtypeStruct((t, LANES), F32),
                   jax.ShapeDtypeStruct((SSD_N_HEADS, t), F32)],
        scratch_shapes=[pltpu.VMEM((2, ROW_HALO + tm, CONV_CHUNK), F32)],
        compiler_params=pltpu.CompilerParams(
            dimension_semantics=("arbitrary",), vmem_limit_bytes=VMEM_LIMIT_BYTES),
        name="norm_in_projection",
    )(x2d, x2d, norm_w, w_all, w_qT, w_vT, w_dtT, conv_w, conv_b.reshape(1, -1), dtb_row, dtb_col,
      cos_t, sa_t, sb_t, cosT_t, sinT_t)


def _ssd_kernel(zs_ref, xc_ref, dt_ref, dtT_ref, alog_row_ref, alog_col_ref, dexp_ref, nw_ref, expand_ref,
                y_ref, state_ref, *, chunks_per_step):
    L = SSD_CHUNK
    P = SSD_HEAD_DIM
    N = SSD_D_STATE
    heads_per_group = SSD_N_HEADS // SSD_N_GROUPS

    @pl.when(pl.program_id(1) == 0)
    def _():
        state_ref[...] = jnp.zeros_like(state_ref)

    neg_a_row = -jnp.exp(alog_row_ref[...])
    neg_a_col = -jnp.exp(alog_col_ref[...])
    ri = lax.broadcasted_iota(jnp.int32, (L, L), 0)
    ci = lax.broadcasted_iota(jnp.int32, (L, L), 1)
    causal = ri >= ci
    tri = jnp.where(causal, 1.0, 0.0).astype(BF16)
    triT = jnp.where(ri <= ci, 1.0, 0.0).astype(BF16)
    lane = lax.broadcasted_iota(jnp.int32, (L, LANES), 1)
    first_half = lane < P
    gw = D_MODEL // SSD_N_GROUPS

    def one_chunk(step, carry):
        rows = pl.ds(pl.multiple_of(step * L, L), L)
        xs_b = xc_ref[rows, :D_MODEL]
        xs = xs_b.astype(F32)

        dt = dt_ref[rows, :]
        dtT = dtT_ref[:, rows]
        dA = dt * neg_a_row
        dAT = dtT * neg_a_col
        a_cs = sum(jnp.dot(tri, part, preferred_element_type=F32) for part in _split3(dA))
        a_csT = sum(jnp.dot(part, triT, preferred_element_type=F32) for part in _split3(dAT))
        a_last = a_cs[L - 1:L, :]
        exp_a = jnp.exp(a_cs)
        w_state = dt * jnp.exp(a_last - a_cs)

        expand = expand_ref[...]
        w_state_x = sum(jnp.dot(part, expand, preferred_element_type=F32) for part in _split2(w_state))
        a_tail = jnp.broadcast_to(a_last, (SUBLANES, LANES))
        a_last_x = sum(jnp.dot(part, expand, preferred_element_type=F32) for part in _split3(a_tail))[0:1, :]
        xd_b = (xs * w_state_x).astype(BF16)

        y_parts = []
        new_state_parts = []
        for g in range(SSD_N_GROUPS):
            b_gb = xc_ref[rows, D_MODEL + g * N:D_MODEL + (g + 1) * N]
            c_gb = xc_ref[rows, D_MODEL + SSD_N_GROUPS * N + g * N:D_MODEL + SSD_N_GROUPS * N + (g + 1) * N]
            c_g = c_gb.astype(F32)
            cb = lax.dot_general(c_gb, b_gb, (((1,), (1,)), ((), ())), preferred_element_type=F32)
            for pair in range(heads_per_group // 2):
                col0 = (g * heads_per_group + 2 * pair) * P
                x_pair = xs_b[:, col0:col0 + 2 * P]
                prev_pair = state_ref[:, col0:col0 + 2 * P].astype(BF16)
                outs = []
                for k in range(2):
                    hd = g * heads_per_group + 2 * pair + k
                    seg = a_cs[:, hd:hd + 1] - a_csT[hd:hd + 1, :]
                    decay = jnp.exp(jnp.where(causal, seg, -jnp.inf))
                    m_h = (cb * decay * dtT[hd:hd + 1, :]).astype(BF16)
                    c_h = (c_g * exp_a[:, hd:hd + 1]).astype(BF16)
                    outs.append(jnp.dot(m_h, x_pair, preferred_element_type=F32)
                                + jnp.dot(c_h, prev_pair, preferred_element_type=F32))
                y_parts.append(jnp.where(first_half, outs[0], outs[1]))
            lo = g * heads_per_group * P
            hi = (g + 1) * heads_per_group * P
            new_state_parts.append(lax.dot_general(b_gb, xd_b[:, lo:hi], (((0,), (0,)), ((), ())),
                                                   preferred_element_type=F32))
        state_ref[...] = state_ref[...] * jnp.exp(a_last_x) + jnp.concatenate(new_state_parts, axis=1)

        y = jnp.concatenate(y_parts, axis=1) + xs * dexp_ref[...]
        yg = y * zs_ref[rows, :].astype(F32)
        nw = nw_ref[...]
        for g in range(SSD_N_GROUPS):
            part = yg[:, g * gw:(g + 1) * gw]
            y_ref[rows, g * gw:(g + 1) * gw] = _rms(part, nw[:, g * gw:(g + 1) * gw], NORM_EPS).astype(BF16)
        return carry

    lax.fori_loop(0, chunks_per_step, one_chunk, 0, unroll=True)


def _ssd_branch(zs, xc, dt, dtT, a_log, d_skip, norm_w, batch, seq):
    L = SSD_CHUNK
    nc = seq // L
    t = batch * seq
    alog_row = jnp.pad(a_log, (0, LANES - SSD_N_HEADS)).reshape(1, LANES)
    alog_col = a_log.reshape(SSD_N_HEADS, 1)
    dexp = jnp.repeat(d_skip, SSD_HEAD_DIM).reshape(1, D_MODEL)
    expand = (jnp.arange(LANES)[:, None] == (jnp.arange(D_MODEL)[None, :] // SSD_HEAD_DIM)).astype(BF16)
    cps = _tile(nc, 4)
    ns = nc // cps
    rows = cps * L
    row = lambda b, c: (b * ns + c, 0)
    return pl.pallas_call(
        functools.partial(_ssd_kernel, chunks_per_step=cps),
        grid=(batch, ns),
        in_specs=[
            pl.BlockSpec((rows, D_MODEL), row),
            pl.BlockSpec((rows, SSD_CONV_DIM), row),
            pl.BlockSpec((rows, LANES), row),
            pl.BlockSpec((SSD_N_HEADS, rows), lambda b, c: (0, b * ns + c)),
            _const_spec((1, LANES)),
            _const_spec((SSD_N_HEADS, 1)),
            _const_spec((1, D_MODEL)),
            _const_spec((1, D_MODEL)),
            _const_spec((LANES, D_MODEL)),
        ],
        out_specs=pl.BlockSpec((rows, D_MODEL), row),
        out_shape=jax.ShapeDtypeStruct((t, D_MODEL), BF16),
        scratch_shapes=[pltpu.VMEM((SSD_D_STATE, D_MODEL), F32)],
        compiler_params=pltpu.CompilerParams(
            dimension_semantics=("arbitrary", "arbitrary"), vmem_limit_bytes=VMEM_LIMIT_BYTES),
        name="ssd_branch",
    )(zs, xc, dt, dtT, alog_row, alog_col, dexp, norm_w.reshape(1, -1), expand)


def _attn_kernel(qT_ref, k_ref, vT_ref, lq1_ref, lk1_ref, lq2_ref, lk2_ref, swc_ref, o_ref,
                 s_ref, p_ref, alpha_ref, m_ref, acc_ref, qm_ref, vext_ref, *, seq, tq):
    d = ATTN_HEAD_DIM
    dv = 2 * d
    tk = tq
    nq = seq // tq
    n_off = nq * (nq - 1) // 2
    assert n_off % 2 == 0 and nq % 2 == 0, "step pairs need even counts"
    neg_inf = -jnp.inf

    p_ref[...] = jnp.zeros(p_ref.shape, BF16)
    alpha_ref[...] = jnp.ones(alpha_ref.shape, F32)
    m_ref[...] = jnp.full(m_ref.shape, neg_inf, F32)
    acc_ref[...] = jnp.zeros(acc_ref.shape, F32)
    key_i = lax.broadcasted_iota(jnp.int32, (tk, LANES), 0)
    qry_i = lax.broadcasted_iota(jnp.int32, (tk, LANES), 1)
    qT_all = qT_ref[...]
    feat = lax.broadcasted_iota(jnp.int32, qT_all.shape, 0)
    zero = jnp.zeros_like(qT_all)
    qm_ref[0] = jnp.where(feat < d, qT_all, zero)
    qm_ref[1] = jnp.where(feat >= d, qT_all, zero)
    vext_ref[0:dv, :] = vT_ref[...]
    vext_ref[dv:, :] = jnp.ones((ONES_ROWS, seq), BF16)
    lam = (jnp.exp(jnp.sum(lq1_ref[...] * lk1_ref[...], axis=-1, keepdims=True))
           - jnp.exp(jnp.sum(lq2_ref[...] * lk2_ref[...], axis=-1, keepdims=True)) + LAMBDA_INIT)
    sw_col = swc_ref[...]

    def stage_scores(step, slot):
        i, j = step
        kb = k_ref[pl.ds(pl.multiple_of(j * tk, tk), tk), :]
        for mp in range(2):
            s_ref[slot, mp] = jnp.dot(kb, qm_ref[mp, :, pl.ds(pl.multiple_of(i * tq, tq), tq)],
                                      preferred_element_type=F32)

    def stage_softmax(step, slot, diagonal):
        i, _ = step
        for mp in range(2):
            for c in range(tq // LANES):
                cs = slice(c * LANES, (c + 1) * LANES)

                def scores():
                    ss = s_ref[slot, mp, :, cs]
                    if diagonal:
                        ss = jnp.where(key_i <= qry_i + c * LANES, ss, neg_inf)
                    return ss

                m_prev = m_ref[i, mp, :, cs]
                m_new = jnp.maximum(m_prev, jnp.max(scores(), axis=0, keepdims=True))
                p_ref[slot, mp, :, cs] = jnp.exp2(scores() - m_new).astype(BF16)
                alpha_ref[slot, mp, :, cs] = jnp.exp2(m_prev - m_new)
                m_ref[i, mp, :, cs] = m_new

    def stage_values(step, slot):
        i, j = step
        v_ext = vext_ref[:, pl.ds(pl.multiple_of(j * tk, tk), tk)]
        for mp in range(2):
            acc_ref[i, mp] = (alpha_ref[slot, mp] * acc_ref[i, mp]
                              + jnp.dot(v_ext, p_ref[slot, mp], preferred_element_type=F32))

    def following(step):
        i, j = step
        on_diag = j == i
        same_row = j + 1 < i
        off_i = jnp.where(same_row, i, i + 1)
        off_j = jnp.where(same_row, j + 1, 0)
        to_diag = off_i >= nq
        off_i = jnp.where(to_diag, 0, off_i)
        off_j = jnp.where(to_diag, 0, off_j)
        diag_i = jnp.minimum(i + 1, nq - 1)
        return jnp.where(on_diag, diag_i, off_i), jnp.where(on_diag, diag_i, off_j)

    first = (jnp.int32(1), jnp.int32(0)) if nq > 1 else (jnp.int32(0), jnp.int32(0))
    stage_scores(first, 0)
    stage_scores(following(first), 1)
    idle = (jnp.int32(0), jnp.int32(0))

    def make_body(diagonal):
        def body(_, carry):
            done2, done1, cur = carry
            nxt1 = following(cur)
            nxt2 = following(nxt1)
            nxt3 = following(nxt2)
            stage_values(done2, 0)
            stage_softmax(cur, 0, diagonal)
            stage_scores(nxt2, 0)
            stage_values(done1, 1)
            stage_softmax(nxt1, 1, diagonal)
            stage_scores(nxt3, 1)
            return cur, nxt1, nxt2
        return body

    carry = lax.fori_loop(0, n_off // 2, make_body(False), (idle, idle, first), unroll=True)
    done2, done1, _ = lax.fori_loop(0, nq // 2, make_body(True), carry, unroll=True)
    stage_values(done2, 0)
    stage_values(done1, 1)

    for i in range(nq):
        inv_l = [1.0 / acc_ref[i, mp, dv:dv + 1, :] for mp in range(2)]
        oT = acc_ref[i, 0, 0:dv, :] * inv_l[0] - lam * (acc_ref[i, 1, 0:dv, :] * inv_l[1])
        ms = jnp.mean(oT * oT, axis=0, keepdims=True)
        y = oT * lax.rsqrt(ms + SUBLN_EPS) * sw_col * (1.0 - LAMBDA_INIT)
        o_ref[i * tq:(i + 1) * tq, :] = y.T.astype(BF16)


def _diff_attention(qT, k, vT, lq1, lk1, lq2, lk2, subln_w, batch, seq, tq):
    t = batch * seq
    vec = lambda a: a.reshape(1, -1)
    rows_blk = pl.BlockSpec((seq, LANES), lambda b, h: (b, h))
    cols_blk = pl.BlockSpec((LANES, seq), lambda b, h: (h, b))
    return pl.pallas_call(
        functools.partial(_attn_kernel, seq=seq, tq=tq),
        grid=(batch, ATTN_N_HEADS),
        in_specs=[
            cols_blk, rows_blk, cols_blk,
            _const_spec((1, ATTN_HEAD_DIM)),
            _const_spec((1, ATTN_HEAD_DIM)),
            _const_spec((1, ATTN_HEAD_DIM)),
            _const_spec((1, ATTN_HEAD_DIM)),
            _const_spec((2 * ATTN_HEAD_DIM, 1)),
        ],
        out_specs=rows_blk,
        out_shape=jax.ShapeDtypeStruct((t, D_MODEL), BF16),
        scratch_shapes=[
            pltpu.VMEM((2, 2, tq, tq), F32),
            pltpu.VMEM((2, 2, tq, tq), BF16),
            pltpu.VMEM((2, 2, 1, tq), F32),
            pltpu.VMEM((seq // tq, 2, 1, tq), F32),
            pltpu.VMEM((seq // tq, 2, LANES + ONES_ROWS, tq), F32),
            pltpu.VMEM((2, LANES, seq), BF16),
            pltpu.VMEM((LANES + ONES_ROWS, seq), BF16),
        ],
        compiler_params=pltpu.CompilerParams(
            dimension_semantics=("arbitrary", "arbitrary"),
            vmem_limit_bytes=VMEM_LIMIT_BYTES),
        name="diff_attention",
    )(qT, k, vT, vec(lq1), vec(lk1), vec(lq2), vec(lk2), subln_w.reshape(-1, 1))


def _merge_kernel(x_ref, ys_ref, ya_ref, gs_ref, ga_ref, wbs_ref, wba_ref, wo_ref, o_ref):
    a = jnp.dot(ys_ref[...], wbs_ref[...], preferred_element_type=F32)
    b = jnp.dot(ya_ref[...], wba_ref[...], preferred_element_type=F32)
    merged = (jax.nn.sigmoid(gs_ref[...].astype(F32)) * a
              + jax.nn.sigmoid(ga_ref[...].astype(F32)) * b)
    o_ref[...] = x_ref[...] + jnp.dot(merged.astype(BF16), wo_ref[...], preferred_element_type=F32)


def _merge(x2d, ys, ya, gs, ga, wbs, wba, wo, tm):
    t, d = x2d.shape
    row = lambda i: (i, 0)
    tile = pl.BlockSpec((tm, d), row)
    return pl.pallas_call(
        _merge_kernel,
        grid=(t // tm,),
        in_specs=[tile, tile, tile, tile, tile,
                  _const_spec((d, d)), _const_spec((d, d)), _const_spec((d, d))],
        out_specs=tile,
        out_shape=jax.ShapeDtypeStruct((t, d), F32),
        compiler_params=pltpu.CompilerParams(
            dimension_semantics=("arbitrary",), vmem_limit_bytes=VMEM_LIMIT_BYTES),
        name="gated_merge_out_projection",
    )(x2d, ys, ya, gs, ga, wbs, wba, wo)


def _ffn_kernel(x_ref, halo_ref, nw_ref, wup_ref, cw_ref, cb_ref, wd_ref, fw_ref, o_ref, gbuf_ref, act_ref,
                *, tm, tiles_per_seq, chunk):
    i = pl.program_id(0)
    x1 = x_ref[...]
    x_ext = jnp.concatenate([halo_ref[...], x1], axis=0)
    h_ext = _rms(x_ext, nw_ref[...], NORM_EPS).astype(BF16)
    h2 = h_ext[FFN_HALO:, :]
    keep_halo = jnp.where(i % tiles_per_seq == 0, 0.0, 1.0)
    cw = cw_ref[...]
    cb = cb_ref[...]
    KW = FFN_CONV_WIDTH
    n_chunks = D_FF // chunk

    def up(c):
        lo = c * chunk
        gate_ext = jnp.dot(h_ext, wup_ref[:, lo:lo + chunk], preferred_element_type=F32)
        val = jnp.dot(h2, wup_ref[:, D_FF + lo:D_FF + lo + chunk], preferred_element_type=F32)
        return gate_ext, val

    acc = jnp.zeros((tm, D_MODEL), F32)
    nxt = up(0)
    for c in range(n_chunks):
        lo = c * chunk
        gate_ext, val = nxt
        if c + 1 < n_chunks:
            nxt = up(c + 1)
        slot = c % 2
        gbuf_ref[slot, 0:FFN_HALO, :] = gate_ext[0:FFN_HALO, :] * keep_halo
        gbuf_ref[slot, FFN_HALO:, :] = gate_ext[FFN_HALO:, :]
        conv = cb[:, lo:lo + chunk] + cw[KW - 1:KW, lo:lo + chunk] * gate_ext[FFN_HALO:, :]
        for j in range(KW - 1):
            shift = KW - 1 - j
            conv = conv + cw[j:j + 1, lo:lo + chunk] * gbuf_ref[slot, pl.ds(FFN_HALO - shift, tm), :]
        act_ref[:, lo:lo + chunk] = (_silu(conv) * val).astype(BF16)
        if (c + 1) % DOWN_GROUP == 0 or c + 1 == n_chunks:
            k0 = (c // DOWN_GROUP) * DOWN_GROUP * chunk
            acc = acc + jnp.dot(act_ref[:, k0:lo + chunk], wd_ref[k0:lo + chunk, :], preferred_element_type=F32)
    o_ref[...] = _rms(x1 + acc, fw_ref[...], NORM_EPS)


def _conv_ffn(x1, norm_w, w_up, conv_w, conv_b, w_down, final_w, seq, tm, chunk):
    t, d = x1.shape
    tiles_per_seq = seq // tm
    halo_blocks = tm // FFN_HALO
    return pl.pallas_call(
        functools.partial(_ffn_kernel, tm=tm, tiles_per_seq=tiles_per_seq, chunk=chunk),
        grid=(t // tm,),
        in_specs=[
            pl.BlockSpec((tm, d), lambda i: (i, 0)),
            pl.BlockSpec((FFN_HALO, d), lambda i: (jnp.maximum(i * halo_blocks - 1, 0), 0)),
            _const_spec((1, d)),
            _const_spec(w_up.shape),
            _const_spec(conv_w.shape),
            _const_spec((1, D_FF)),
            _const_spec(w_down.shape),
            _const_spec((1, d)),
        ],
        out_specs=pl.BlockSpec((tm, d), lambda i: (i, 0)),
        out_shape=jax.ShapeDtypeStruct((t, d), F32),
        scratch_shapes=[pltpu.VMEM((2, FFN_HALO + tm, chunk), F32), pltpu.VMEM((tm, D_FF), BF16)],
        compiler_params=pltpu.CompilerParams(
            dimension_semantics=("arbitrary",), vmem_limit_bytes=VMEM_LIMIT_BYTES),
        name="conv_ffn_final_norm",
    )(x1, x1, norm_w.reshape(1, -1), w_up, conv_w, conv_b.reshape(1, -1), w_down, final_w.reshape(1, -1))


def _rope_tables(seq):
    half = ATTN_HEAD_DIM // 2
    inv = 1.0 / (ROPE_THETA ** (jnp.arange(0, ATTN_HEAD_DIM, 2, dtype=F32) / ATTN_HEAD_DIM))
    ang = jnp.arange(seq, dtype=F32)[:, None] * inv[None, :]
    lane = jnp.arange(LANES)
    cos = jnp.cos(ang)[:, lane % half]
    sin = jnp.sin(ang)[:, lane % half]
    second = (lane % ATTN_HEAD_DIM) >= half
    sa = jnp.where(second[None, :], sin, 0.0)
    sb = jnp.where(second[None, :], 0.0, -sin)
    return cos, sa, sb, jnp.cos(ang).T, jnp.sin(ang).T


def _tile(n, pref):
    while n % pref:
        pref //= 2
    return pref


def kernel(x, norm_mix_w, w_in, ssd_conv_w, ssd_conv_b, ssd_dt_bias, ssd_a_log, ssd_d_skip, ssd_norm_w, lambda_q1, lambda_k1, lambda_q2, lambda_k2, subln_w, w_branch_ssd, w_branch_attn, w_out, norm_ffn_w, w_up, ffn_conv_w, ffn_conv_b, w_down, final_norm_w):
    batch, seq, d = x.shape
    assert d == D_MODEL and seq % SSD_CHUNK == 0
    t = batch * seq
    x2d = x.reshape(t, d)
    li = 0

    w = w_in[li]
    o_z, o_xbc = 0, D_MODEL
    o_dt = o_xbc + SSD_CONV_DIM
    o_q = o_dt + SSD_N_HEADS
    cols = lambda lo, n: w[:, lo:lo + n]
    w_dt = cols(o_dt, SSD_N_HEADS)
    w_all = jnp.concatenate([
        cols(o_z, D_MODEL), cols(o_q + D_MODEL, D_MODEL),
        cols(o_q + 3 * D_MODEL, D_MODEL), cols(o_q + 4 * D_MODEL, D_MODEL), cols(o_xbc, SSD_CONV_DIM),
        jnp.pad(w_dt, ((0, 0), (0, LANES - SSD_N_HEADS)))], axis=1).astype(BF16)
    assert w_all.shape[1] == _IN_COLS_PADDED
    w_dtT = w_dt.T.astype(BF16)
    w_qT = cols(o_q, D_MODEL).T.astype(BF16)
    w_vT = cols(o_q + 2 * D_MODEL, D_MODEL).T.astype(BF16)

    tm = _tile(seq, 512)
    zs, qT, k, vT, gs, ga, xc, dt, dtT = _in_projection(
        x2d, norm_mix_w[li].reshape(1, -1), w_all, w_qT, w_vT, w_dtT, ssd_conv_w[li], ssd_conv_b[li],
        ssd_dt_bias[li], _rope_tables(seq), seq, tm)

    y_ssd = _ssd_branch(zs, xc, dt, dtT, ssd_a_log[li], ssd_d_skip[li], ssd_norm_w[li], batch, seq)
    tq = _tile(seq, 256)
    y_attn = _diff_attention(qT, k, vT, lambda_q1[li], lambda_k1[li], lambda_q2[li], lambda_k2[li],
                             subln_w[li], batch, seq, tq)
    x1 = _merge(x2d, y_ssd, y_attn, gs, ga, w_branch_ssd[li].astype(BF16), w_branch_attn[li].astype(BF16),
                w_out[li].astype(BF16), tm)
    out = _conv_ffn(x1, norm_ffn_w[li], w_up[li].astype(BF16), ffn_conv_w[li], ffn_conv_b[li],
                    w_down[li].astype(BF16), final_norm_w, seq, tm, 256)
    return out.reshape(batch, seq, d)
```

```python
import functools
import math

import jax
import jax.numpy as jnp
from jax import lax
from jax.experimental import pallas as pl
from jax.experimental.pallas import tpu as pltpu

F32 = jnp.float32
BF16 = jnp.bfloat16

D_MODEL = 1024
SSD_HEAD_DIM = 64
SSD_N_HEADS = 16
SSD_N_GROUPS = 2
SSD_D_STATE = 128
SSD_CONV_WIDTH = 4
SSD_CHUNK = 128
SSD_CONV_DIM = D_MODEL + 2 * SSD_N_GROUPS * SSD_D_STATE
ATTN_HEAD_DIM = 64
ATTN_N_HEADS = 8
ROPE_THETA = 10000.0
D_FF = 2816
FFN_CONV_WIDTH = 3
NORM_EPS = 1e-6
SUBLN_EPS = 1e-5
LAMBDA_INIT = 0.8 - 0.6 * math.exp(-0.3 * 0)

LANES = 128
SUBLANES = 8
ONES_ROWS = 16
ROW_HALO = 16
FFN_HALO = ROW_HALO
CONV_CHUNK = 256
DOWN_GROUP = 4
VMEM_LIMIT_BYTES = 56 * 1024 * 1024

_OFF_Z, _OFF_K, _OFF_GS, _OFF_GA = (i * D_MODEL for i in range(4))
_OFF_XBC = 4 * D_MODEL
_OFF_DT = _OFF_XBC + SSD_CONV_DIM
_IN_COLS_PADDED = _OFF_DT + LANES


def _const_spec(shape):
    nd = len(shape)
    return pl.BlockSpec(shape, lambda *_: (0,) * nd, pipeline_mode=pl.Buffered(1))


def _rms(x, w, eps):
    return x * lax.rsqrt(jnp.mean(x * x, axis=-1, keepdims=True) + eps) * w


def _split2(x):
    hi = x.astype(BF16)
    lo = (x - hi.astype(F32)).astype(BF16)
    return hi, lo


def _split3(x):
    hi = x.astype(BF16)
    r = x - hi.astype(F32)
    mid = r.astype(BF16)
    lo = (r - mid.astype(F32)).astype(BF16)
    return hi, mid, lo


def _softplus(x):
    return jnp.maximum(x, 0.0) + jnp.log1p(jnp.exp(-jnp.abs(x)))


def _silu(x):
    return x * jax.nn.sigmoid(x)


def _inproj_kernel(x_ref, halo_ref, nw_ref, w_ref, wqT_ref, wvT_ref, wdt_ref, cw_ref, cb_ref, dtb_row_ref,
                   dtb_col_ref, cos_ref, sa_ref, sb_ref, cosT_ref, sinT_ref,
                   zs_ref, qT_ref, k_ref, vT_ref, gs_ref, ga_ref, xc_ref, dt_ref, dtT_ref, cbuf_ref,
                   *, tm, tiles_per_seq):
    i = pl.program_id(0)
    x_ext = jnp.concatenate([halo_ref[...], x_ref[...]], axis=0)
    h_ext = _rms(x_ext, nw_ref[...], NORM_EPS).astype(BF16)
    h = h_ext[ROW_HALO:, :]
    keep_halo = jnp.where(i % tiles_per_seq == 0, 0.0, 1.0)
    nt = (((1,), (1,)), ((), ()))

    def mm(lo, width):
        return jnp.dot(h, w_ref[:, lo:lo + width], preferred_element_type=F32)

    half = ATTN_HEAD_DIM // 2

    def emit_zs():
        zs_ref[...] = _silu(mm(_OFF_Z, D_MODEL)).astype(BF16)

    def emit_gates_dt():
        gs_ref[...] = mm(_OFF_GS, D_MODEL).astype(BF16)
        dt_ref[...] = _softplus(mm(_OFF_DT, LANES) + dtb_row_ref[...])
        dtT_ref[...] = _softplus(lax.dot_general(wdt_ref[...], h, nt, preferred_element_type=F32)
                                 + dtb_col_ref[...])

    def emit_ga():
        ga_ref[...] = mm(_OFF_GA, D_MODEL).astype(BF16)

    def emit_vT():
        vT_ref[...] = lax.dot_general(wvT_ref[...], h, nt, preferred_element_type=F32).astype(BF16)

    def emit_k():
        cos, sa, sb = cos_ref[...], sa_ref[...], sb_ref[...]
        acc = mm(_OFF_K, D_MODEL)
        for j in range(D_MODEL // LANES):
            blk = acc[:, j * LANES:(j + 1) * LANES]
            r = blk * cos + pltpu.roll(blk, half, 1) * sa + pltpu.roll(blk, LANES - half, 1) * sb
            k_ref[:, j * LANES:(j + 1) * LANES] = r.astype(BF16)

    def emit_qT():
        scale = ATTN_HEAD_DIM ** -0.5 * math.log2(math.e)
        cosT, sinT = cosT_ref[...] * scale, sinT_ref[...] * scale
        accT = lax.dot_general(wqT_ref[...], h, nt, preferred_element_type=F32)
        for g in range(D_MODEL // ATTN_HEAD_DIM):
            lo = g * ATTN_HEAD_DIM
            x1 = accT[lo:lo + half, :]
            x2 = accT[lo + half:lo + 2 * half, :]
            qT_ref[lo:lo + half, :] = (x1 * cosT - x2 * sinT).astype(BF16)
            qT_ref[lo + half:lo + 2 * half, :] = (x2 * cosT + x1 * sinT).astype(BF16)

    cw = cw_ref[...]
    cb = cb_ref[...]
    KW = SSD_CONV_WIDTH
    n_chunks = SSD_CONV_DIM // CONV_CHUNK
    fillers = [emit_zs, emit_gates_dt, emit_ga, emit_vT, emit_k, emit_qT]
    assert len(fillers) == n_chunks

    def xbc_product(c):
        lo = _OFF_XBC + c * CONV_CHUNK
        return jnp.dot(h_ext, w_ref[:, lo:lo + CONV_CHUNK], preferred_element_type=F32)

    nxt = xbc_product(0)
    for c in range(n_chunks):
        cols = slice(c * CONV_CHUNK, (c + 1) * CONV_CHUNK)
        xe = nxt
        if c + 1 < n_chunks:
            nxt = xbc_product(c + 1)
        fillers[c]()
        slot = c % 2
        cbuf_ref[slot, 0:ROW_HALO, :] = xe[0:ROW_HALO, :] * keep_halo
        cbuf_ref[slot, ROW_HALO:, :] = xe[ROW_HALO:, :]
        conv = cb[:, cols] + cw[KW - 1:KW, cols] * xe[ROW_HALO:, :]
        for j in range(KW - 1):
            shift = KW - 1 - j
            conv = conv + cw[j:j + 1, cols] * cbuf_ref[slot, pl.ds(ROW_HALO - shift, tm), :]
        xc_ref[:, cols] = _silu(conv).astype(BF16)


def _in_projection(x2d, norm_w, w_all, w_qT, w_vT, w_dtT, conv_w, conv_b, dt_bias, rope, seq, tm):
    t, d = x2d.shape
    n_seq_tiles = seq // tm
    cos_t, sa_t, sb_t, cosT_t, sinT_t = rope
    half = ATTN_HEAD_DIM // 2
    halo_blocks = tm // ROW_HALO
    row = lambda i: (i, 0)
    colblk = lambda i: (0, i)
    tab = lambda i: (i % n_seq_tiles, 0)
    tabT = lambda i: (0, i % n_seq_tiles)
    big = jax.ShapeDtypeStruct((t, D_MODEL), BF16)
    bigT = jax.ShapeDtypeStruct((D_MODEL, t), BF16)
    dtb_row = jnp.pad(dt_bias, (0, LANES - SSD_N_HEADS)).reshape(1, LANES)
    dtb_col = dt_bias.reshape(SSD_N_HEADS, 1)
    return pl.pallas_call(
        functools.partial(_inproj_kernel, tm=tm, tiles_per_seq=n_seq_tiles),
        grid=(t // tm,),
        in_specs=[
            pl.BlockSpec((tm, d), row),
            pl.BlockSpec((ROW_HALO, d), lambda i: (jnp.maximum(i * halo_blocks - 1, 0), 0)),
            _const_spec((1, d)),
            _const_spec(w_all.shape),
            _const_spec(w_qT.shape),
            _const_spec(w_vT.shape),
            _const_spec(w_dtT.shape),
            _const_spec((SSD_CONV_WIDTH, SSD_CONV_DIM)),
            _const_spec((1, SSD_CONV_DIM)),
            _const_spec((1, LANES)),
            _const_spec((SSD_N_HEADS, 1)),
            pl.BlockSpec((tm, LANES), tab),
            pl.BlockSpec((tm, LANES), tab),
            pl.BlockSpec((tm, LANES), tab),
            pl.BlockSpec((half, tm), tabT),
            pl.BlockSpec((half, tm), tabT),
        ],
        out_specs=[
            pl.BlockSpec((tm, D_MODEL), row),
            pl.BlockSpec((D_MODEL, tm), colblk),
            pl.BlockSpec((tm, D_MODEL), row),
            pl.BlockSpec((D_MODEL, tm), colblk),
            pl.BlockSpec((tm, D_MODEL), row),
            pl.BlockSpec((tm, D_MODEL), row),
            pl.BlockSpec((tm, SSD_CONV_DIM), row),
            pl.BlockSpec((tm, LANES), row),
            pl.BlockSpec((SSD_N_HEADS, tm), colblk),
        ],
        out_shape=[big, bigT, big, bigT, big, big,
                   jax.ShapeDtypeStruct((t, SSD_CONV_DIM), BF16),
                   jax.ShapeDtypeStruct((t, LANES), F32),
                   jax.ShapeDtypeStruct((SSD_N_HEADS, t), F32)],
        scratch_shapes=[pltpu.VMEM((2, ROW_HALO + tm, CONV_CHUNK), F32)],
        compiler_params=pltpu.CompilerParams(
            dimension_semantics=("arbitrary",), vmem_limit_bytes=VMEM_LIMIT_BYTES),
        name="norm_in_projection",
    )(x2d, x2d, norm_w, w_all, w_qT, w_vT, w_dtT, conv_w, conv_b.reshape(1, -1), dtb_row, dtb_col,
      cos_t, sa_t, sb_t, cosT_t, sinT_t)


def _ssd_kernel(zs_ref, xc_ref, dt_ref, dtT_ref, alog_row_ref, alog_col_ref, dexp_ref, nw_ref, expand_ref,
                y_ref, state_ref, *, chunks_per_step):
    L = SSD_CHUNK
    P = SSD_HEAD_DIM
    N = SSD_D_STATE
    heads_per_group = SSD_N_HEADS // SSD_N_GROUPS

    @pl.when(pl.program_id(1) == 0)
    def _():
        state_ref[...] = jnp.zeros_like(state_ref)

    neg_a_row = -jnp.exp(alog_row_ref[...])
    neg_a_col = -jnp.exp(alog_col_ref[...])
    ri = lax.broadcasted_iota(jnp.int32, (L, L), 0)
    ci = lax.broadcasted_iota(jnp.int32, (L, L), 1)
    causal = ri >= ci
    tri = jnp.where(causal, 1.0, 0.0).astype(BF16)
    triT = jnp.where(ri <= ci, 1.0, 0.0).astype(BF16)
    lane = lax.broadcasted_iota(jnp.int32, (L, LANES), 1)
    first_half = lane < P
    gw = D_MODEL // SSD_N_GROUPS

    def one_chunk(step, carry):
        rows = pl.ds(pl.multiple_of(step * L, L), L)
        xs_b = xc_ref[rows, :D_MODEL]
        xs = xs_b.astype(F32)

        dt = dt_ref[rows, :]
        dtT = dtT_ref[:, rows]
        dA = dt * neg_a_row
        dAT = dtT * neg_a_col
        a_cs = sum(jnp.dot(tri, part, preferred_element_type=F32) for part in _split3(dA))
        a_csT = sum(jnp.dot(part, triT, preferred_element_type=F32) for part in _split3(dAT))
        a_last = a_cs[L - 1:L, :]
        w_state = dt * jnp.exp(a_last - a_cs)

        expand = expand_ref[...]
        w_state_x = sum(jnp.dot(part, expand, preferred_element_type=F32) for part in _split2(w_state))
        a_tail = jnp.broadcast_to(a_last, (SUBLANES, LANES))
        a_last_x = sum(jnp.dot(part, expand, preferred_element_type=F32) for part in _split3(a_tail))[0:1, :]
        decay_tot = jnp.exp(a_last_x)
        assert gw == heads_per_group * P
        for g in range(SSD_N_GROUPS):
            gcols = slice(g * gw, (g + 1) * gw)
            b_gb = xc_ref[rows, D_MODEL + g * N:D_MODEL + (g + 1) * N]
            c_gb = xc_ref[rows, D_MODEL + SSD_N_GROUPS * N + g * N:D_MODEL + SSD_N_GROUPS * N + (g + 1) * N]
            c_g = c_gb.astype(F32)
            cb = lax.dot_general(c_gb, b_gb, (((1,), (1,)), ((), ())), preferred_element_type=F32)
            y_parts = []
            for pair in range(heads_per_group // 2):
                col0 = (g * heads_per_group + 2 * pair) * P
                x_pair = xs_b[:, col0:col0 + 2 * P]
                prev_pair = state_ref[:, col0:col0 + 2 * P].astype(BF16)
                outs = []
                for k in range(2):
                    hd = g * heads_per_group + 2 * pair + k
                    a_col = jnp.broadcast_to(a_cs[:, hd:hd + 1], (L, L))
                    seg = a_col - a_csT[hd:hd + 1, :]
                    decay = jnp.exp(jnp.where(causal, seg, -jnp.inf))
                    m_h = (cb * decay * dtT[hd:hd + 1, :]).astype(BF16)
                    c_h = (c_g * jnp.exp(a_col)).astype(BF16)
                    outs.append(jnp.dot(m_h, x_pair, preferred_element_type=F32)
                                + jnp.dot(c_h, prev_pair, preferred_element_type=F32))
                y_parts.append(jnp.where(first_half, outs[0], outs[1]))
            xs_g = xs[:, gcols]
            xd_g = (xs_g * w_state_x[:, gcols]).astype(BF16)
            state_ref[:, gcols] = (state_ref[:, gcols] * decay_tot[:, gcols]
                                   + lax.dot_general(b_gb, xd_g, (((0,), (0,)), ((), ())),
                                                     preferred_element_type=F32))
            y_g = jnp.concatenate(y_parts, axis=1) + xs_g * dexp_ref[:, gcols]
            yg = y_g * zs_ref[rows, gcols].astype(F32)
            y_ref[rows, gcols] = _rms(yg, nw_ref[:, gcols], NORM_EPS).astype(BF16)
        return carry

    lax.fori_loop(0, chunks_per_step, one_chunk, 0, unroll=True)


def _ssd_branch(zs, xc, dt, dtT, a_log, d_skip, norm_w, batch, seq):
    L = SSD_CHUNK
    nc = seq // L
    t = batch * seq
    alog_row = jnp.pad(a_log, (0, LANES - SSD_N_HEADS)).reshape(1, LANES)
    alog_col = a_log.reshape(SSD_N_HEADS, 1)
    dexp = jnp.repeat(d_skip, SSD_HEAD_DIM).reshape(1, D_MODEL)
    expand = (jnp.arange(LANES)[:, None] == (jnp.arange(D_MODEL)[None, :] // SSD_HEAD_DIM)).astype(BF16)
    cps = _tile(nc, 4)
    ns = nc // cps
    rows = cps * L
    row = lambda b, c: (b * ns + c, 0)
    return pl.pallas_call(
        functools.partial(_ssd_kernel, chunks_per_step=cps),
        grid=(batch, ns),
        in_specs=[
            pl.BlockSpec((rows, D_MODEL), row),
            pl.BlockSpec((rows, SSD_CONV_DIM), row),
            pl.BlockSpec((rows, LANES), row),
            pl.BlockSpec((SSD_N_HEADS, rows), lambda b, c: (0, b * ns + c)),
            _const_spec((1, LANES)),
            _const_spec((SSD_N_HEADS, 1)),
            _const_spec((1, D_MODEL)),
            _const_spec((1, D_MODEL)),
            _const_spec((LANES, D_MODEL)),
        ],
        out_specs=pl.BlockSpec((rows, D_MODEL), row),
        out_shape=jax.ShapeDtypeStruct((t, D_MODEL), BF16),
        scratch_shapes=[pltpu.VMEM((SSD_D_STATE, D_MODEL), F32)],
        compiler_params=pltpu.CompilerParams(
            dimension_semantics=("arbitrary", "arbitrary"), vmem_limit_bytes=VMEM_LIMIT_BYTES),
        name="ssd_branch",
    )(zs, xc, dt, dtT, alog_row, alog_col, dexp, norm_w.reshape(1, -1), expand)


def _attn_kernel(qT_ref, k_ref, vT_ref, lq1_ref, lk1_ref, lq2_ref, lk2_ref, swc_ref, o_ref,
                 s_ref, p_ref, alpha_ref, m_ref, acc_ref, qm_ref, vext_ref, *, seq, tq):
    d = ATTN_HEAD_DIM
    dv = 2 * d
    tk = tq
    nq = seq // tq
    neg_inf = -jnp.inf

    key_i = lax.broadcasted_iota(jnp.int32, (tk, LANES), 0)
    qry_i = lax.broadcasted_iota(jnp.int32, (tk, LANES), 1)
    qT_all = qT_ref[...]
    feat = lax.broadcasted_iota(jnp.int32, qT_all.shape, 0)
    zero = jnp.zeros_like(qT_all)
    qm_ref[0] = jnp.where(feat < d, qT_all, zero)
    qm_ref[1] = jnp.where(feat >= d, qT_all, zero)
    vext_ref[0:dv, :] = vT_ref[...]
    vext_ref[dv:, :] = jnp.ones((ONES_ROWS, seq), BF16)
    lam = (jnp.exp(jnp.sum(lq1_ref[...] * lk1_ref[...], axis=-1, keepdims=True))
           - jnp.exp(jnp.sum(lq2_ref[...] * lk2_ref[...], axis=-1, keepdims=True)) + LAMBDA_INIT)
    sw_col = swc_ref[...]

    steps = [(i, j) for i in range(1, nq) for j in range(i)] + [(i, i) for i in range(nq)]
    visited = set()
    first_visit = []
    for i, _ in steps:
        first_visit.append(i not in visited)
        visited.add(i)

    def stage_scores(t):
        i, j = steps[t]
        kb = k_ref[j * tk:(j + 1) * tk, :]
        for mp in range(2):
            s_ref[t % 2, mp] = jnp.dot(kb, qm_ref[mp, :, i * tq:(i + 1) * tq], preferred_element_type=F32)

    def stage_softmax(t):
        i, j = steps[t]
        slot = t % 2
        for mp in range(2):
            for c in range(tq // LANES):
                cs = slice(c * LANES, (c + 1) * LANES)

                def scores():
                    ss = s_ref[slot, mp, :, cs]
                    if i == j:
                        ss = jnp.where(key_i <= qry_i + c * LANES, ss, neg_inf)
                    return ss

                m_new = jnp.max(scores(), axis=0, keepdims=True)
                if not first_visit[t]:
                    m_prev = m_ref[i, mp, :, cs]
                    m_new = jnp.maximum(m_prev, m_new)
                    alpha_ref[slot, mp, :, cs] = jnp.exp2(m_prev - m_new)
                p_ref[slot, mp, :, cs] = jnp.exp2(scores() - m_new).astype(BF16)
                m_ref[i, mp, :, cs] = m_new

    def stage_values(t):
        i, j = steps[t]
        slot = t % 2
        v_ext = vext_ref[:, j * tk:(j + 1) * tk]
        for mp in range(2):
            pv = jnp.dot(v_ext, p_ref[slot, mp], preferred_element_type=F32)
            if first_visit[t]:
                acc_ref[i, mp] = pv
            else:
                acc_ref[i, mp] = alpha_ref[slot, mp] * acc_ref[i, mp] + pv

    n = len(steps)
    stage_scores(0)
    if n > 1:
        stage_scores(1)
    for t in range(n + 2):
        if t >= 2:
            stage_values(t - 2)
        if t < n:
            stage_softmax(t)
        if t + 2 < n:
            stage_scores(t + 2)

    for i in range(nq):
        inv_l = [1.0 / acc_ref[i, mp, dv:dv + 1, :] for mp in range(2)]
        oT = acc_ref[i, 0, 0:dv, :] * inv_l[0] - lam * (acc_ref[i, 1, 0:dv, :] * inv_l[1])
        ms = jnp.mean(oT * oT, axis=0, keepdims=True)
        y = oT * lax.rsqrt(ms + SUBLN_EPS) * sw_col * (1.0 - LAMBDA_INIT)
        o_ref[i * tq:(i + 1) * tq, :] = y.T.astype(BF16)


def _diff_attention(qT, k, vT, lq1, lk1, lq2, lk2, subln_w, batch, seq, tq):
    t = batch * seq
    vec = lambda a: a.reshape(1, -1)
    rows_blk = pl.BlockSpec((seq, LANES), lambda b, h: (b, h))
    cols_blk = pl.BlockSpec((LANES, seq), lambda b, h: (h, b))
    return pl.pallas_call(
        functools.partial(_attn_kernel, seq=seq, tq=tq),
        grid=(batch, ATTN_N_HEADS),
        in_specs=[
            cols_blk, rows_blk, cols_blk,
            _const_spec((1, ATTN_HEAD_DIM)),
            _const_spec((1, ATTN_HEAD_DIM)),
            _const_spec((1, ATTN_HEAD_DIM)),
            _const_spec((1, ATTN_HEAD_DIM)),
            _const_spec((2 * ATTN_HEAD_DIM, 1)),
        ],
        out_specs=rows_blk,
        out_shape=jax.ShapeDtypeStruct((t, D_MODEL), BF16),
        scratch_shapes=[
            pltpu.VMEM((2, 2, tq, tq), F32),
            pltpu.VMEM((2, 2, tq, tq), BF16),
            pltpu.VMEM((2, 2, 1, tq), F32),
            pltpu.VMEM((seq // tq, 2, 1, tq), F32),
            pltpu.VMEM((seq // tq, 2, LANES + ONES_ROWS, tq), F32),
            pltpu.VMEM((2, LANES, seq), BF16),
            pltpu.VMEM((LANES + ONES_ROWS, seq), BF16),
        ],
        compiler_params=pltpu.CompilerParams(
            dimension_semantics=("arbitrary", "arbitrary"),
            vmem_limit_bytes=VMEM_LIMIT_BYTES),
        name="diff_attention",
    )(qT, k, vT, vec(lq1), vec(lk1), vec(lq2), vec(lk2), subln_w.reshape(-1, 1))


def _merge_kernel(x_ref, ys_ref, ya_ref, gs_ref, ga_ref, wbs_ref, wba_ref, wo_ref, o_ref):
    a = jnp.dot(ys_ref[...], wbs_ref[...], preferred_element_type=F32)
    b = jnp.dot(ya_ref[...], wba_ref[...], preferred_element_type=F32)
    merged = (jax.nn.sigmoid(gs_ref[...].astype(F32)) * a
              + jax.nn.sigmoid(ga_ref[...].astype(F32)) * b)
    o_ref[...] = x_ref[...] + jnp.dot(merged.astype(BF16), wo_ref[...], preferred_element_type=F32)


def _merge(x2d, ys, ya, gs, ga, wbs, wba, wo, tm):
    t, d = x2d.shape
    row = lambda i: (i, 0)
    tile = pl.BlockSpec((tm, d), row)
    return pl.pallas_call(
        _merge_kernel,
        grid=(t // tm,),
        in_specs=[tile, tile, tile, tile, tile,
                  _const_spec((d, d)), _const_spec((d, d)), _const_spec((d, d))],
        out_specs=tile,
        out_shape=jax.ShapeDtypeStruct((t, d), F32),
        compiler_params=pltpu.CompilerParams(
            dimension_semantics=("arbitrary",), vmem_limit_bytes=VMEM_LIMIT_BYTES),
        name="gated_merge_out_projection",
    )(x2d, ys, ya, gs, ga, wbs, wba, wo)


def _ffn_kernel(x_ref, halo_ref, nw_ref, wup_ref, cw_ref, cb_ref, wd_ref, fw_ref, o_ref, gbuf_ref, act_ref,
                *, tm, tiles_per_seq, chunk):
    i = pl.program_id(0)
    x1 = x_ref[...]
    x_ext = jnp.concatenate([halo_ref[...], x1], axis=0)
    h_ext = _rms(x_ext, nw_ref[...], NORM_EPS).astype(BF16)
    h2 = h_ext[FFN_HALO:, :]
    keep_halo = jnp.where(i % tiles_per_seq == 0, 0.0, 1.0)
    cw = cw_ref[...]
    cb = cb_ref[...]
    KW = FFN_CONV_WIDTH
    n_chunks = D_FF // chunk

    def up(c):
        lo = c * chunk
        gate_ext = jnp.dot(h_ext, wup_ref[:, lo:lo + chunk], preferred_element_type=F32)
        val = jnp.dot(h2, wup_ref[:, D_FF + lo:D_FF + lo + chunk], preferred_element_type=F32)
        return gate_ext, val

    acc = jnp.zeros((tm, D_MODEL), F32)
    nxt = up(0)
    for c in range(n_chunks):
        lo = c * chunk
        gate_ext, val = nxt
        if c + 1 < n_chunks:
            nxt = up(c + 1)
        slot = c % 2
        gbuf_ref[slot, 0:FFN_HALO, :] = gate_ext[0:FFN_HALO, :] * keep_halo
        gbuf_ref[slot, FFN_HALO:, :] = gate_ext[FFN_HALO:, :]
        conv = cb[:, lo:lo + chunk] + cw[KW - 1:KW, lo:lo + chunk] * gate_ext[FFN_HALO:, :]
        for j in range(KW - 1):
            shift = KW - 1 - j
            conv = conv + cw[j:j + 1, lo:lo + chunk] * gbuf_ref[slot, pl.ds(FFN_HALO - shift, tm), :]
        act_ref[:, lo:lo + chunk] = (_silu(conv) * val).astype(BF16)
        if (c + 1) % DOWN_GROUP == 0 or c + 1 == n_chunks:
            k0 = (c // DOWN_GROUP) * DOWN_GROUP * chunk
            acc = acc + jnp.dot(act_ref[:, k0:lo + chunk], wd_ref[k0:lo + chunk, :], preferred_element_type=F32)
    o_ref[...] = _rms(x1 + acc, fw_ref[...], NORM_EPS)


def _conv_ffn(x1, norm_w, w_up, conv_w, conv_b, w_down, final_w, seq, tm, chunk):
    t, d = x1.shape
    tiles_per_seq = seq // tm
    halo_blocks = tm // FFN_HALO
    return pl.pallas_call(
        functools.partial(_ffn_kernel, tm=tm, tiles_per_seq=tiles_per_seq, chunk=chunk),
        grid=(t // tm,),
        in_specs=[
            pl.BlockSpec((tm, d), lambda i: (i, 0)),
            pl.BlockSpec((FFN_HALO, d), lambda i: (jnp.maximum(i * halo_blocks - 1, 0), 0)),
            _const_spec((1, d)),
            _const_spec(w_up.shape),
            _const_spec(conv_w.shape),
            _const_spec((1, D_FF)),
            _const_spec(w_down.shape),
            _const_spec((1, d)),
        ],
        out_specs=pl.BlockSpec((tm, d), lambda i: (i, 0)),
        out_shape=jax.ShapeDtypeStruct((t, d), F32),
        scratch_shapes=[pltpu.VMEM((2, FFN_HALO + tm, chunk), F32), pltpu.VMEM((tm, D_FF), BF16)],
        compiler_params=pltpu.CompilerParams(
            dimension_semantics=("arbitrary",), vmem_limit_bytes=VMEM_LIMIT_BYTES),
        name="conv_ffn_final_norm",
    )(x1, x1, norm_w.reshape(1, -1), w_up, conv_w, conv_b.reshape(1, -1), w_down, final_w.reshape(1, -1))


def _rope_tables(seq):
    half = ATTN_HEAD_DIM // 2
    inv = 1.0 / (ROPE_THETA ** (jnp.arange(0, ATTN_HEAD_DIM, 2, dtype=F32) / ATTN_HEAD_DIM))
    ang = jnp.arange(seq, dtype=F32)[:, None] * inv[None, :]
    lane = jnp.arange(LANES)
    cos = jnp.cos(ang)[:, lane % half]
    sin = jnp.sin(ang)[:, lane % half]
    second = (lane % ATTN_HEAD_DIM) >= half
    sa = jnp.where(second[None, :], sin, 0.0)
    sb = jnp.where(second[None, :], 0.0, -sin)
    return cos, sa, sb, jnp.cos(ang).T, jnp.sin(ang).T


def _tile(n, pref):
    while n % pref:
        pref //= 2
    return pref


def kernel(x, norm_mix_w, w_in, ssd_conv_w, ssd_conv_b, ssd_dt_bias, ssd_a_log, ssd_d_skip, ssd_norm_w, lambda_q1, lambda_k1, lambda_q2, lambda_k2, subln_w, w_branch_ssd, w_branch_attn, w_out, norm_ffn_w, w_up, ffn_conv_w, ffn_conv_b, w_down, final_norm_w):
    batch, seq, d = x.shape
    assert d == D_MODEL and seq % SSD_CHUNK == 0
    t = batch * seq
    x2d = x.reshape(t, d)
    li = 0

    w = w_in[li]
    o_z, o_xbc = 0, D_MODEL
    o_dt = o_xbc + SSD_CONV_DIM
    o_q = o_dt + SSD_N_HEADS
    cols = lambda lo, n: w[:, lo:lo + n]
    w_dt = cols(o_dt, SSD_N_HEADS)
    w_all = jnp.concatenate([
        cols(o_z, D_MODEL), cols(o_q + D_MODEL, D_MODEL),
        cols(o_q + 3 * D_MODEL, D_MODEL), cols(o_q + 4 * D_MODEL, D_MODEL), cols(o_xbc, SSD_CONV_DIM),
        jnp.pad(w_dt, ((0, 0), (0, LANES - SSD_N_HEADS)))], axis=1).astype(BF16)
    assert w_all.shape[1] == _IN_COLS_PADDED
    w_dtT = w_dt.T.astype(BF16)
    w_qT = cols(o_q, D_MODEL).T.astype(BF16)
    w_vT = cols(o_q + 2 * D_MODEL, D_MODEL).T.astype(BF16)

    tm = _tile(seq, 512)
    zs, qT, k, vT, gs, ga, xc, dt, dtT = _in_projection(
        x2d, norm_mix_w[li].reshape(1, -1), w_all, w_qT, w_vT, w_dtT, ssd_conv_w[li], ssd_conv_b[li],
        ssd_dt_bias[li], _rope_tables(seq), seq, tm)

    y_ssd = _ssd_branch(zs, xc, dt, dtT, ssd_a_log[li], ssd_d_skip[li], ssd_norm_w[li], batch, seq)
    tq = _tile(seq, 256)
    y_attn = _diff_attention(qT, k, vT, lambda_q1[li], lambda_k1[li], lambda_q2[li], lambda_k2[li],
                             subln_w[li], batch, seq, tq)
    x1 = _merge(x2d, y_ssd, y_attn, gs, ga, w_branch_ssd[li].astype(BF16), w_branch_attn[li].astype(BF16),
                w_out[li].astype(BF16), _tile(seq, 1024))
    out = _conv_ffn(x1, norm_ffn_w[li], w_up[li].astype(BF16), ffn_conv_w[li], ffn_conv_b[li],
                    w_down[li].astype(BF16), final_norm_w, seq, _tile(seq, 1024), 256)
    return out.reshape(batch, seq, d)
```

```python
import functools
import math

import jax
import jax.numpy as jnp
from jax import lax
from jax.experimental import pallas as pl
from jax.experimental.pallas import tpu as pltpu

F32 = jnp.float32
BF16 = jnp.bfloat16

D_MODEL = 1024
SSD_HEAD_DIM = 64
SSD_N_HEADS = 16
SSD_N_GROUPS = 2
SSD_D_STATE = 128
SSD_CONV_WIDTH = 4
SSD_CHUNK = 128
SSD_CONV_DIM = D_MODEL + 2 * SSD_N_GROUPS * SSD_D_STATE
ATTN_HEAD_DIM = 64
ATTN_N_HEADS = 8
ROPE_THETA = 10000.0
D_FF = 2816
FFN_CONV_WIDTH = 3
NORM_EPS = 1e-6
SUBLN_EPS = 1e-5
LAMBDA_INIT = 0.8 - 0.6 * math.exp(-0.3 * 0)

LANES = 128
SUBLANES = 8
ONES_ROWS = 16
ROW_HALO = 16
FFN_HALO = ROW_HALO
CONV_CHUNK = 256
DOWN_GROUP = 4
VMEM_LIMIT_BYTES = 56 * 1024 * 1024

_OFF_Z, _OFF_K, _OFF_GS, _OFF_GA = (i * D_MODEL for i in range(4))
_OFF_XBC = 4 * D_MODEL
_OFF_DT = _OFF_XBC + SSD_CONV_DIM
_IN_COLS_PADDED = _OFF_DT + LANES


def _const_spec(shape):
    nd = len(shape)
    return pl.BlockSpec(shape, lambda *_: (0,) * nd, pipeline_mode=pl.Buffered(1))


def _rms(x, w, eps):
    return x * lax.rsqrt(jnp.mean(x * x, axis=-1, keepdims=True) + eps) * w


def _split2(x):
    hi = x.astype(BF16)
    lo = (x - hi.astype(F32)).astype(BF16)
    return hi, lo


def _split3(x):
    hi = x.astype(BF16)
    r = x - hi.astype(F32)
    mid = r.astype(BF16)
    lo = (r - mid.astype(F32)).astype(BF16)
    return hi, mid, lo


def _softplus(x):
    return jnp.maximum(x, 0.0) + jnp.log1p(jnp.exp(-jnp.abs(x)))


def _silu(x):
    return x * jax.nn.sigmoid(x)


def _inproj_kernel(x_ref, halo_ref, nw_ref, w_ref, wqT_ref, wvT_ref, wdt_ref, cw_ref, cb_ref, dtb_row_ref,
                   dtb_col_ref, cos_ref, sa_ref, sb_ref, cosT_ref, sinT_ref,
                   zs_ref, qT_ref, k_ref, vT_ref, gs_ref, ga_ref, xc_ref, dt_ref, dtT_ref, cbuf_ref,
                   *, tm, tiles_per_seq):
    i = pl.program_id(0)
    x_ext = jnp.concatenate([halo_ref[...], x_ref[...]], axis=0)
    h_ext = _rms(x_ext, nw_ref[...], NORM_EPS).astype(BF16)
    h = h_ext[ROW_HALO:, :]
    keep_halo = jnp.where(i % tiles_per_seq == 0, 0.0, 1.0)
    nt = (((1,), (1,)), ((), ()))

    def mm(lo, width):
        return jnp.dot(h, w_ref[:, lo:lo + width], preferred_element_type=F32)

    half = ATTN_HEAD_DIM // 2

    def emit_zs():
        zs_ref[...] = _silu(mm(_OFF_Z, D_MODEL)).astype(BF16)

    def emit_gates_dt():
        gs_ref[...] = mm(_OFF_GS, D_MODEL).astype(BF16)
        dt_ref[...] = _softplus(mm(_OFF_DT, LANES) + dtb_row_ref[...])
        dtT_ref[...] = _softplus(lax.dot_general(wdt_ref[...], h, nt, preferred_element_type=F32)
                                 + dtb_col_ref[...])

    def emit_ga():
        ga_ref[...] = mm(_OFF_GA, D_MODEL).astype(BF16)

    def emit_vT():
        vT_ref[...] = lax.dot_general(wvT_ref[...], h, nt, preferred_element_type=F32).astype(BF16)

    def emit_k():
        cos, sa, sb = cos_ref[...], sa_ref[...], sb_ref[...]
        acc = mm(_OFF_K, D_MODEL)
        for j in range(D_MODEL // LANES):
            blk = acc[:, j * LANES:(j + 1) * LANES]
            r = blk * cos + pltpu.roll(blk, half, 1) * sa + pltpu.roll(blk, LANES - half, 1) * sb
            k_ref[:, j * LANES:(j + 1) * LANES] = r.astype(BF16)

    def emit_qT():
        scale = ATTN_HEAD_DIM ** -0.5 * math.log2(math.e)
        cosT, sinT = cosT_ref[...] * scale, sinT_ref[...] * scale
        accT = lax.dot_general(wqT_ref[...], h, nt, preferred_element_type=F32)
        for g in range(D_MODEL // ATTN_HEAD_DIM):
            lo = g * ATTN_HEAD_DIM
            x1 = accT[lo:lo + half, :]
            x2 = accT[lo + half:lo + 2 * half, :]
            qT_ref[lo:lo + half, :] = (x1 * cosT - x2 * sinT).astype(BF16)
            qT_ref[lo + half:lo + 2 * half, :] = (x2 * cosT + x1 * sinT).astype(BF16)

    cw = cw_ref[...]
    cb = cb_ref[...]
    KW = SSD_CONV_WIDTH
    n_chunks = SSD_CONV_DIM // CONV_CHUNK
    fillers = [emit_zs, emit_gates_dt, emit_ga, emit_vT, emit_k, emit_qT]
    assert len(fillers) == n_chunks

    def xbc_product(c):
        lo = _OFF_XBC + c * CONV_CHUNK
        return jnp.dot(h_ext, w_ref[:, lo:lo + CONV_CHUNK], preferred_element_type=F32)

    nxt = xbc_product(0)
    for c in range(n_chunks):
        cols = slice(c * CONV_CHUNK, (c + 1) * CONV_CHUNK)
        xe = nxt
        if c + 1 < n_chunks:
            nxt = xbc_product(c + 1)
        fillers[c]()
        slot = c % 2
        cbuf_ref[slot, 0:ROW_HALO, :] = xe[0:ROW_HALO, :] * keep_halo
        cbuf_ref[slot, ROW_HALO:, :] = xe[ROW_HALO:, :]
        conv = cb[:, cols] + cw[KW - 1:KW, cols] * xe[ROW_HALO:, :]
        for j in range(KW - 1):
            shift = KW - 1 - j
            conv = conv + cw[j:j + 1, cols] * cbuf_ref[slot, pl.ds(ROW_HALO - shift, tm), :]
        xc_ref[:, cols] = _silu(conv).astype(BF16)


def _in_projection(x2d, norm_w, w_all, w_qT, w_vT, w_dtT, conv_w, conv_b, dt_bias, rope, seq, tm):
    t, d = x2d.shape
    n_seq_tiles = seq // tm
    cos_t, sa_t, sb_t, cosT_t, sinT_t = rope
    half = ATTN_HEAD_DIM // 2
    halo_blocks = tm // ROW_HALO
    row = lambda i: (i, 0)
    colblk = lambda i: (0, i)
    tab = lambda i: (i % n_seq_tiles, 0)
    tabT = lambda i: (0, i % n_seq_tiles)
    big = jax.ShapeDtypeStruct((t, D_MODEL), BF16)
    bigT = jax.ShapeDtypeStruct((D_MODEL, t), BF16)
    dtb_row = jnp.pad(dt_bias, (0, LANES - SSD_N_HEADS)).reshape(1, LANES)
    dtb_col = dt_bias.reshape(SSD_N_HEADS, 1)
    return pl.pallas_call(
        functools.partial(_inproj_kernel, tm=tm, tiles_per_seq=n_seq_tiles),
        grid=(t // tm,),
        in_specs=[
            pl.BlockSpec((tm, d), row),
            pl.BlockSpec((ROW_HALO, d), lambda i: (jnp.maximum(i * halo_blocks - 1, 0), 0)),
            _const_spec((1, d)),
            _const_spec(w_all.shape),
            _const_spec(w_qT.shape),
            _const_spec(w_vT.shape),
            _const_spec(w_dtT.shape),
            _const_spec((SSD_CONV_WIDTH, SSD_CONV_DIM)),
            _const_spec((1, SSD_CONV_DIM)),
            _const_spec((1, LANES)),
            _const_spec((SSD_N_HEADS, 1)),
            pl.BlockSpec((tm, LANES), tab),
            pl.BlockSpec((tm, LANES), tab),
            pl.BlockSpec((tm, LANES), tab),
            pl.BlockSpec((half, tm), tabT),
            pl.BlockSpec((half, tm), tabT),
        ],
        out_specs=[
            pl.BlockSpec((tm, D_MODEL), row),
            pl.BlockSpec((D_MODEL, tm), colblk),
            pl.BlockSpec((tm, D_MODEL), row),
            pl.BlockSpec((D_MODEL, tm), colblk),
            pl.BlockSpec((tm, D_MODEL), row),
            pl.BlockSpec((tm, D_MODEL), row),
            pl.BlockSpec((tm, SSD_CONV_DIM), row),
            pl.BlockSpec((tm, LANES), row),
            pl.BlockSpec((SSD_N_HEADS, tm), colblk),
        ],
        out_shape=[big, bigT, big, bigT, big, big,
                   jax.ShapeDtypeStruct((t, SSD_CONV_DIM), BF16),
                   jax.ShapeDtypeStruct((t, LANES), F32),
                   jax.ShapeDtypeStruct((SSD_N_HEADS, t), F32)],
        scratch_shapes=[pltpu.VMEM((2, ROW_HALO + tm, CONV_CHUNK), F32)],
        compiler_params=pltpu.CompilerParams(
            dimension_semantics=("arbitrary",), vmem_limit_bytes=VMEM_LIMIT_BYTES),
        name="norm_in_projection",
    )(x2d, x2d, norm_w, w_all, w_qT, w_vT, w_dtT, conv_w, conv_b.reshape(1, -1), dtb_row, dtb_col,
      cos_t, sa_t, sb_t, cosT_t, sinT_t)


def _ssd_kernel(zs_ref, xc_ref, dt_ref, dtT_ref, alog_row_ref, alog_col_ref, dexp_ref, nw_ref, expand_ref,
                y_ref, state_ref, *, chunks_per_step):
    L = SSD_CHUNK
    P = SSD_HEAD_DIM
    N = SSD_D_STATE
    heads_per_group = SSD_N_HEADS // SSD_N_GROUPS

    @pl.when(pl.program_id(1) == 0)
    def _():
        state_ref[...] = jnp.zeros_like(state_ref)

    neg_a_row = -jnp.exp(alog_row_ref[...])
    neg_a_col = -jnp.exp(alog_col_ref[...])
    ri = lax.broadcasted_iota(jnp.int32, (L, L), 0)
    ci = lax.broadcasted_iota(jnp.int32, (L, L), 1)
    causal = ri >= ci
    tri = jnp.where(causal, 1.0, 0.0).astype(BF16)
    triT = jnp.where(ri <= ci, 1.0, 0.0).astype(BF16)
    lane = lax.broadcasted_iota(jnp.int32, (L, LANES), 1)
    first_half = lane < P
    gw = D_MODEL // SSD_N_GROUPS

    def one_chunk(step, carry):
        rows = pl.ds(pl.multiple_of(step * L, L), L)
        xs_b = xc_ref[rows, :D_MODEL]
        xs = xs_b.astype(F32)

        dt = dt_ref[rows, :]
        dtT = dtT_ref[:, rows]
        dA = dt * neg_a_row
        dAT = dtT * neg_a_col
        a_cs = sum(jnp.dot(tri, part, preferred_element_type=F32) for part in _split3(dA))
        a_csT = sum(jnp.dot(part, triT, preferred_element_type=F32) for part in _split3(dAT))
        a_last = a_cs[L - 1:L, :]
        w_state = dt * jnp.exp(a_last - a_cs)

        expand = expand_ref[...]
        w_state_x = sum(jnp.dot(part, expand, preferred_element_type=F32) for part in _split2(w_state))
        a_tail = jnp.broadcast_to(a_last, (SUBLANES, LANES))
        a_last_x = sum(jnp.dot(part, expand, preferred_element_type=F32) for part in _split3(a_tail))[0:1, :]
        decay_tot = jnp.exp(a_last_x)
        assert gw == heads_per_group * P
        for g in range(SSD_N_GROUPS):
            gcols = slice(g * gw, (g + 1) * gw)
            b_gb = xc_ref[rows, D_MODEL + g * N:D_MODEL + (g + 1) * N]
            c_gb = xc_ref[rows, D_MODEL + SSD_N_GROUPS * N + g * N:D_MODEL + SSD_N_GROUPS * N + (g + 1) * N]
            c_g = c_gb.astype(F32)
            cb = lax.dot_general(c_gb, b_gb, (((1,), (1,)), ((), ())), preferred_element_type=F32)
            y_parts = []
            for pair in range(heads_per_group // 2):
                col0 = (g * heads_per_group + 2 * pair) * P
                x_pair = xs_b[:, col0:col0 + 2 * P]
                prev_pair = state_ref[:, col0:col0 + 2 * P].astype(BF16)
                outs = []
                for k in range(2):
                    hd = g * heads_per_group + 2 * pair + k
                    a_col = jnp.broadcast_to(a_cs[:, hd:hd + 1], (L, L))
                    seg = a_col - a_csT[hd:hd + 1, :]
                    decay = jnp.exp(jnp.where(causal, seg, -jnp.inf))
                    m_h = (cb * decay * dtT[hd:hd + 1, :]).astype(BF16)
                    c_h = (c_g * jnp.exp(a_col)).astype(BF16)
                    outs.append(jnp.dot(m_h, x_pair, preferred_element_type=F32)
                                + jnp.dot(c_h, prev_pair, preferred_element_type=F32))
                y_parts.append(jnp.where(first_half, outs[0], outs[1]))
            xs_g = xs[:, gcols]
            xd_g = (xs_g * w_state_x[:, gcols]).astype(BF16)
            state_ref[:, gcols] = (state_ref[:, gcols] * decay_tot[:, gcols]
                                   + lax.dot_general(b_gb, xd_g, (((0,), (0,)), ((), ())),
                                                     preferred_element_type=F32))
            y_g = jnp.concatenate(y_parts, axis=1) + xs_g * dexp_ref[:, gcols]
            yg = y_g * zs_ref[rows, gcols].astype(F32)
            y_ref[rows, gcols] = _rms(yg, nw_ref[:, gcols], NORM_EPS).astype(BF16)
        return carry

    lax.fori_loop(0, chunks_per_step, one_chunk, 0, unroll=True)


def _ssd_branch(zs, xc, dt, dtT, a_log, d_skip, norm_w, batch, seq):
    L = SSD_CHUNK
    nc = seq // L
    t = batch * seq
    alog_row = jnp.pad(a_log, (0, LANES - SSD_N_HEADS)).reshape(1, LANES)
    alog_col = a_log.reshape(SSD_N_HEADS, 1)
    dexp = jnp.repeat(d_skip, SSD_HEAD_DIM).reshape(1, D_MODEL)
    expand = (jnp.arange(LANES)[:, None] == (jnp.arange(D_MODEL)[None, :] // SSD_HEAD_DIM)).astype(BF16)
    cps = _tile(nc, 8)
    ns = nc // cps
    rows = cps * L
    row = lambda b, c: (b * ns + c, 0)
    return pl.pallas_call(
        functools.partial(_ssd_kernel, chunks_per_step=cps),
        grid=(batch, ns),
        in_specs=[
            pl.BlockSpec((rows, D_MODEL), row),
            pl.BlockSpec((rows, SSD_CONV_DIM), row),
            pl.BlockSpec((rows, LANES), row),
            pl.BlockSpec((SSD_N_HEADS, rows), lambda b, c: (0, b * ns + c)),
            _const_spec((1, LANES)),
            _const_spec((SSD_N_HEADS, 1)),
            _const_spec((1, D_MODEL)),
            _const_spec((1, D_MODEL)),
            _const_spec((LANES, D_MODEL)),
        ],
        out_specs=pl.BlockSpec((rows, D_MODEL), row),
        out_shape=jax.ShapeDtypeStruct((t, D_MODEL), BF16),
        scratch_shapes=[pltpu.VMEM((SSD_D_STATE, D_MODEL), F32)],
        compiler_params=pltpu.CompilerParams(
            dimension_semantics=("arbitrary", "arbitrary"), vmem_limit_bytes=VMEM_LIMIT_BYTES),
        name="ssd_branch",
    )(zs, xc, dt, dtT, alog_row, alog_col, dexp, norm_w.reshape(1, -1), expand)


def _attn_kernel(qT_ref, k_ref, vT_ref, lq1_ref, lk1_ref, lq2_ref, lk2_ref, swc_ref, o_ref,
                 s_ref, p_ref, alpha_ref, m_ref, acc_ref, qm_ref, vext_ref, *, seq, tq):
    d = ATTN_HEAD_DIM
    dv = 2 * d
    tk = tq
    nq = seq // tq
    neg_inf = -jnp.inf

    key_i = lax.broadcasted_iota(jnp.int32, (tk, LANES), 0)
    qry_i = lax.broadcasted_iota(jnp.int32, (tk, LANES), 1)
    n_lt = tq // LANES
    feat = lax.broadcasted_iota(jnp.int32, (dv, LANES), 0)
    ones_rows = jnp.ones((ONES_ROWS, LANES), BF16)
    for ct in range(seq // LANES):
        q_c = qT_ref[:, ct * LANES:(ct + 1) * LANES]
        zero = jnp.zeros_like(q_c)
        qm_ref[0, ct] = jnp.where(feat < d, q_c, zero)
        qm_ref[1, ct] = jnp.where(feat >= d, q_c, zero)
        vext_ref[ct, 0:dv, :] = vT_ref[:, ct * LANES:(ct + 1) * LANES]
        vext_ref[ct, dv:, :] = ones_rows
    lam = (jnp.exp(jnp.sum(lq1_ref[...] * lk1_ref[...], axis=-1, keepdims=True))
           - jnp.exp(jnp.sum(lq2_ref[...] * lk2_ref[...], axis=-1, keepdims=True)) + LAMBDA_INIT)
    sw_col = swc_ref[...]

    def lane_tiles(ref, lead, first, count):
        return jnp.concatenate([ref[lead + (first + c,)] for c in range(count)], axis=1)

    steps = [(i, j) for i in range(1, nq) for j in range(i)] + [(i, i) for i in range(nq)]
    visited = set()
    first_visit = []
    for i, _ in steps:
        first_visit.append(i not in visited)
        visited.add(i)

    def stage_scores(t):
        i, j = steps[t]
        kb = k_ref[j * tk:(j + 1) * tk, :]
        for mp in range(2):
            s = jnp.dot(kb, lane_tiles(qm_ref, (mp,), i * n_lt, n_lt), preferred_element_type=F32)
            for c in range(n_lt):
                s_ref[t % 2, mp, c] = s[:, c * LANES:(c + 1) * LANES]

    def stage_softmax(t):
        i, j = steps[t]
        slot = t % 2
        for mp in range(2):
            for c in range(n_lt):

                def scores():
                    ss = s_ref[slot, mp, c]
                    if i == j:
                        ss = jnp.where(key_i <= qry_i + c * LANES, ss, neg_inf)
                    return ss

                m_new = jnp.max(scores(), axis=0, keepdims=True)
                if not first_visit[t]:
                    m_prev = m_ref[i, mp, c]
                    m_new = jnp.maximum(m_prev, m_new)
                    alpha_ref[slot, mp, c] = jnp.exp2(m_prev - m_new)
                p_ref[slot, mp, c] = jnp.exp2(scores() - m_new).astype(BF16)
                m_ref[i, mp, c] = m_new

    def stage_values(t):
        i, j = steps[t]
        slot = t % 2
        v_ext = lane_tiles(vext_ref, (), j * n_lt, n_lt)
        for mp in range(2):
            pv = jnp.dot(v_ext, lane_tiles(p_ref, (slot, mp), 0, n_lt), preferred_element_type=F32)
            for c in range(n_lt):
                pv_c = pv[:, c * LANES:(c + 1) * LANES]
                if first_visit[t]:
                    acc_ref[i, mp, c] = pv_c
                else:
                    acc_ref[i, mp, c] = alpha_ref[slot, mp, c] * acc_ref[i, mp, c] + pv_c

    n = len(steps)
    stage_scores(0)
    if n > 1:
        stage_scores(1)
    for t in range(n + 2):
        if t >= 2:
            stage_values(t - 2)
        if t < n:
            stage_softmax(t)
        if t + 2 < n:
            stage_scores(t + 2)

    for i in range(nq):
        for c in range(n_lt):
            inv_l = [1.0 / acc_ref[i, mp, c, dv:dv + 1, :] for mp in range(2)]
            oT = acc_ref[i, 0, c, 0:dv, :] * inv_l[0] - lam * (acc_ref[i, 1, c, 0:dv, :] * inv_l[1])
            ms = jnp.mean(oT * oT, axis=0, keepdims=True)
            y = oT * lax.rsqrt(ms + SUBLN_EPS) * sw_col * (1.0 - LAMBDA_INIT)
            o_ref[i * tq + c * LANES:i * tq + (c + 1) * LANES, :] = y.T.astype(BF16)


def _diff_attention(qT, k, vT, lq1, lk1, lq2, lk2, subln_w, batch, seq, tq):
    t = batch * seq
    vec = lambda a: a.reshape(1, -1)
    rows_blk = pl.BlockSpec((seq, LANES), lambda b, h: (b, h))
    cols_blk = pl.BlockSpec((LANES, seq), lambda b, h: (h, b))
    return pl.pallas_call(
        functools.partial(_attn_kernel, seq=seq, tq=tq),
        grid=(batch, ATTN_N_HEADS),
        in_specs=[
            cols_blk, rows_blk, cols_blk,
            _const_spec((1, ATTN_HEAD_DIM)),
            _const_spec((1, ATTN_HEAD_DIM)),
            _const_spec((1, ATTN_HEAD_DIM)),
            _const_spec((1, ATTN_HEAD_DIM)),
            _const_spec((2 * ATTN_HEAD_DIM, 1)),
        ],
        out_specs=rows_blk,
        out_shape=jax.ShapeDtypeStruct((t, D_MODEL), BF16),
        scratch_shapes=[
            pltpu.VMEM((2, 2, tq // LANES, tq, LANES), F32),
            pltpu.VMEM((2, 2, tq // LANES, tq, LANES), BF16),
            pltpu.VMEM((2, 2, tq // LANES, 1, LANES), F32),
            pltpu.VMEM((seq // tq, 2, tq // LANES, 1, LANES), F32),
            pltpu.VMEM((seq // tq, 2, tq // LANES, LANES + ONES_ROWS, LANES), F32),
            pltpu.VMEM((2, seq // LANES, LANES, LANES), BF16),
            pltpu.VMEM((seq // LANES, LANES + ONES_ROWS, LANES), BF16),
        ],
        compiler_params=pltpu.CompilerParams(
            dimension_semantics=("arbitrary", "arbitrary"),
            vmem_limit_bytes=VMEM_LIMIT_BYTES),
        name="diff_attention",
    )(qT, k, vT, vec(lq1), vec(lk1), vec(lq2), vec(lk2), subln_w.reshape(-1, 1))


def _merge_kernel(x_ref, ys_ref, ya_ref, gs_ref, ga_ref, wbs_ref, wba_ref, wo_ref, o_ref):
    a = jnp.dot(ys_ref[...], wbs_ref[...], preferred_element_type=F32)
    b = jnp.dot(ya_ref[...], wba_ref[...], preferred_element_type=F32)
    merged = (jax.nn.sigmoid(gs_ref[...].astype(F32)) * a
              + jax.nn.sigmoid(ga_ref[...].astype(F32)) * b)
    o_ref[...] = x_ref[...] + jnp.dot(merged.astype(BF16), wo_ref[...], preferred_element_type=F32)


def _merge(x2d, ys, ya, gs, ga, wbs, wba, wo, tm):
    t, d = x2d.shape
    row = lambda i: (i, 0)
    tile = pl.BlockSpec((tm, d), row)
    return pl.pallas_call(
        _merge_kernel,
        grid=(t // tm,),
        in_specs=[tile, tile, tile, tile, tile,
                  _const_spec((d, d)), _const_spec((d, d)), _const_spec((d, d))],
        out_specs=tile,
        out_shape=jax.ShapeDtypeStruct((t, d), F32),
        compiler_params=pltpu.CompilerParams(
            dimension_semantics=("arbitrary",), vmem_limit_bytes=VMEM_LIMIT_BYTES),
        name="gated_merge_out_projection",
    )(x2d, ys, ya, gs, ga, wbs, wba, wo)


def _ffn_kernel(x_ref, halo_ref, nw_ref, wup_ref, cw_ref, cb_ref, wd_ref, fw_ref, o_ref, gbuf_ref, act_ref,
                *, tm, tiles_per_seq, chunk):
    i = pl.program_id(0)
    x1 = x_ref[...]
    x_ext = jnp.concatenate([halo_ref[...], x1], axis=0)
    h_ext = _rms(x_ext, nw_ref[...], NORM_EPS).astype(BF16)
    h2 = h_ext[FFN_HALO:, :]
    keep_halo = jnp.where(i % tiles_per_seq == 0, 0.0, 1.0)
    cw = cw_ref[...]
    cb = cb_ref[...]
    KW = FFN_CONV_WIDTH
    n_chunks = D_FF // chunk

    def up(c):
        lo = c * chunk
        gate_ext = jnp.dot(h_ext, wup_ref[:, lo:lo + chunk], preferred_element_type=F32)
        val = jnp.dot(h2, wup_ref[:, D_FF + lo:D_FF + lo + chunk], preferred_element_type=F32)
        return gate_ext, val

    acc = jnp.zeros((tm, D_MODEL), F32)
    nxt = up(0)
    for c in range(n_chunks):
        lo = c * chunk
        gate_ext, val = nxt
        if c + 1 < n_chunks:
            nxt = up(c + 1)
        slot = c % 2
        gbuf_ref[slot, 0:FFN_HALO, :] = gate_ext[0:FFN_HALO, :] * keep_halo
        gbuf_ref[slot, FFN_HALO:, :] = gate_ext[FFN_HALO:, :]
        conv = cb[:, lo:lo + chunk] + cw[KW - 1:KW, lo:lo + chunk] * gate_ext[FFN_HALO:, :]
        for j in range(KW - 1):
            shift = KW - 1 - j
            conv = conv + cw[j:j + 1, lo:lo + chunk] * gbuf_ref[slot, pl.ds(FFN_HALO - shift, tm), :]
        act_ref[:, lo:lo + chunk] = (_silu(conv) * val).astype(BF16)
        if (c + 1) % DOWN_GROUP == 0 or c + 1 == n_chunks:
            k0 = (c // DOWN_GROUP) * DOWN_GROUP * chunk
            acc = acc + jnp.dot(act_ref[:, k0:lo + chunk], wd_ref[k0:lo + chunk, :], preferred_element_type=F32)
    o_ref[...] = _rms(x1 + acc, fw_ref[...], NORM_EPS)


def _conv_ffn(x1, norm_w, w_up, conv_w, conv_b, w_down, final_w, seq, tm, chunk):
    t, d = x1.shape
    tiles_per_seq = seq // tm
    halo_blocks = tm // FFN_HALO
    return pl.pallas_call(
        functools.partial(_ffn_kernel, tm=tm, tiles_per_seq=tiles_per_seq, chunk=chunk),
        grid=(t // tm,),
        in_specs=[
            pl.BlockSpec((tm, d), lambda i: (i, 0)),
            pl.BlockSpec((FFN_HALO, d), lambda i: (jnp.maximum(i * halo_blocks - 1, 0), 0)),
            _const_spec((1, d)),
            _const_spec(w_up.shape),
            _const_spec(conv_w.shape),
            _const_spec((1, D_FF)),
            _const_spec(w_down.shape),
            _const_spec((1, d)),
        ],
        out_specs=pl.BlockSpec((tm, d), lambda i: (i, 0)),
        out_shape=jax.ShapeDtypeStruct((t, d), F32),
        scratch_shapes=[pltpu.VMEM((2, FFN_HALO + tm, chunk), F32), pltpu.VMEM((tm, D_FF), BF16)],
        compiler_params=pltpu.CompilerParams(
            dimension_semantics=("arbitrary",), vmem_limit_bytes=VMEM_LIMIT_BYTES),
        name="conv_ffn_final_norm",
    )(x1, x1, norm_w.reshape(1, -1), w_up, conv_w, conv_b.reshape(1, -1), w_down, final_w.reshape(1, -1))


def _rope_tables(seq):
    half = ATTN_HEAD_DIM // 2
    inv = 1.0 / (ROPE_THETA ** (jnp.arange(0, ATTN_HEAD_DIM, 2, dtype=F32) / ATTN_HEAD_DIM))
    ang = jnp.arange(seq, dtype=F32)[:, None] * inv[None, :]
    lane = jnp.arange(LANES)
    cos = jnp.cos(ang)[:, lane % half]
    sin = jnp.sin(ang)[:, lane % half]
    second = (lane % ATTN_HEAD_DIM) >= half
    sa = jnp.where(second[None, :], sin, 0.0)
    sb = jnp.where(second[None, :], 0.0, -sin)
    return cos, sa, sb, jnp.cos(ang).T, jnp.sin(ang).T


def _tile(n, pref):
    while n % pref:
        pref //= 2
    return pref


def kernel(x, norm_mix_w, w_in, ssd_conv_w, ssd_conv_b, ssd_dt_bias, ssd_a_log, ssd_d_skip, ssd_norm_w, lambda_q1, lambda_k1, lambda_q2, lambda_k2, subln_w, w_branch_ssd, w_branch_attn, w_out, norm_ffn_w, w_up, ffn_conv_w, ffn_conv_b, w_down, final_norm_w):
    batch, seq, d = x.shape
    assert d == D_MODEL and seq % SSD_CHUNK == 0
    t = batch * seq
    x2d = x.reshape(t, d)
    li = 0

    w = w_in[li]
    o_z, o_xbc = 0, D_MODEL
    o_dt = o_xbc + SSD_CONV_DIM
    o_q = o_dt + SSD_N_HEADS
    cols = lambda lo, n: w[:, lo:lo + n]
    w_dt = cols(o_dt, SSD_N_HEADS)
    w_all = jnp.concatenate([
        cols(o_z, D_MODEL), cols(o_q + D_MODEL, D_MODEL),
        cols(o_q + 3 * D_MODEL, D_MODEL), cols(o_q + 4 * D_MODEL, D_MODEL), cols(o_xbc, SSD_CONV_DIM),
        jnp.pad(w_dt, ((0, 0), (0, LANES - SSD_N_HEADS)))], axis=1).astype(BF16)
    assert w_all.shape[1] == _IN_COLS_PADDED
    w_dtT = w_dt.T.astype(BF16)
    w_qT = cols(o_q, D_MODEL).T.astype(BF16)
    w_vT = cols(o_q + 2 * D_MODEL, D_MODEL).T.astype(BF16)

    tm = _tile(seq, 512)
    zs, qT, k, vT, gs, ga, xc, dt, dtT = _in_projection(
        x2d, norm_mix_w[li].reshape(1, -1), w_all, w_qT, w_vT, w_dtT, ssd_conv_w[li], ssd_conv_b[li],
        ssd_dt_bias[li], _rope_tables(seq), seq, tm)

    y_ssd = _ssd_branch(zs, xc, dt, dtT, ssd_a_log[li], ssd_d_skip[li], ssd_norm_w[li], batch, seq)
    tq = _tile(seq, 256)
    y_attn = _diff_attention(qT, k, vT, lambda_q1[li], lambda_k1[li], lambda_q2[li], lambda_k2[li],
                             subln_w[li], batch, seq, tq)
    x1 = _merge(x2d, y_ssd, y_attn, gs, ga, w_branch_ssd[li].astype(BF16), w_branch_attn[li].astype(BF16),
                w_out[li].astype(BF16), _tile(seq, 1024))
    out = _conv_ffn(x1, norm_ffn_w[li], w_up[li].astype(BF16), ffn_conv_w[li], ffn_conv_b[li],
                    w_down[li].astype(BF16), final_norm_w, seq, _tile(seq, 1024), 256)
    return out.reshape(batch, seq, d)
```

```python
import functools
import math

import jax
import jax.numpy as jnp
from jax import lax
from jax.experimental import pallas as pl
from jax.experimental.pallas import tpu as pltpu

F32 = jnp.float32
BF16 = jnp.bfloat16

D_MODEL = 1024
SSD_HEAD_DIM = 64
SSD_N_HEADS = 16
SSD_N_GROUPS = 2
SSD_D_STATE = 128
SSD_CONV_WIDTH = 4
SSD_CHUNK = 128
SSD_CONV_DIM = D_MODEL + 2 * SSD_N_GROUPS * SSD_D_STATE
ATTN_HEAD_DIM = 64
ATTN_N_HEADS = 8
ROPE_THETA = 10000.0
D_FF = 2816
FFN_CONV_WIDTH = 3
NORM_EPS = 1e-6
SUBLN_EPS = 1e-5
LAMBDA_INIT = 0.8 - 0.6 * math.exp(-0.3 * 0)

LANES = 128
SUBLANES = 8
ONES_ROWS = 16
ROW_HALO = 16
FFN_HALO = ROW_HALO
CONV_CHUNK = 256
DOWN_GROUP = 4
VMEM_LIMIT_BYTES = 56 * 1024 * 1024

_OFF_Z, _OFF_K, _OFF_GS, _OFF_GA = (i * D_MODEL for i in range(4))
_OFF_XBC = 4 * D_MODEL
_OFF_DT = _OFF_XBC + SSD_CONV_DIM
_IN_COLS_PADDED = _OFF_DT + LANES


def _const_spec(shape):
    nd = len(shape)
    return pl.BlockSpec(shape, lambda *_: (0,) * nd, pipeline_mode=pl.Buffered(1))


def _rms(x, w, eps):
    return x * lax.rsqrt(jnp.mean(x * x, axis=-1, keepdims=True) + eps) * w


def _split2(x):
    hi = x.astype(BF16)
    lo = (x - hi.astype(F32)).astype(BF16)
    return hi, lo


def _split3(x):
    hi = x.astype(BF16)
    r = x - hi.astype(F32)
    mid = r.astype(BF16)
    lo = (r - mid.astype(F32)).astype(BF16)
    return hi, mid, lo


def _softplus(x):
    return jnp.maximum(x, 0.0) + jnp.log1p(jnp.exp(-jnp.abs(x)))


def _silu(x):
    return x * jax.nn.sigmoid(x)


def _inproj_kernel(x_ref, halo_ref, nw_ref, w_ref, wqT_ref, wvT_ref, wdt_ref, cw_ref, cb_ref, dtb_row_ref,
                   dtb_col_ref, cos_ref, sa_ref, sb_ref, cosT_ref, sinT_ref,
                   zs_ref, qT_ref, k_ref, vT_ref, gs_ref, ga_ref, xc_ref, dt_ref, dtT_ref, cbuf_ref,
                   *, tm, tiles_per_seq):
    i = pl.program_id(0)
    x_ext = jnp.concatenate([halo_ref[...], x_ref[...]], axis=0)
    h_ext = _rms(x_ext, nw_ref[...], NORM_EPS).astype(BF16)
    h = h_ext[ROW_HALO:, :]
    keep_halo = jnp.where(i % tiles_per_seq == 0, 0.0, 1.0)
    nt = (((1,), (1,)), ((), ()))

    def mm(lo, width):
        return jnp.dot(h, w_ref[:, lo:lo + width], preferred_element_type=F32)

    half = ATTN_HEAD_DIM // 2

    def emit_zs():
        zs_ref[...] = _silu(mm(_OFF_Z, D_MODEL)).astype(BF16)

    def emit_gates_dt():
        gs_ref[...] = mm(_OFF_GS, D_MODEL).astype(BF16)
        dt_ref[...] = _softplus(mm(_OFF_DT, LANES) + dtb_row_ref[...])
        dtT_ref[...] = _softplus(lax.dot_general(wdt_ref[...], h, nt, preferred_element_type=F32)
                                 + dtb_col_ref[...])

    def emit_ga():
        ga_ref[...] = mm(_OFF_GA, D_MODEL).astype(BF16)

    def emit_vT():
        vT_ref[...] = lax.dot_general(wvT_ref[...], h, nt, preferred_element_type=F32).astype(BF16)

    def emit_k():
        cos, sa, sb = cos_ref[...], sa_ref[...], sb_ref[...]
        acc = mm(_OFF_K, D_MODEL)
        for j in range(D_MODEL // LANES):
            blk = acc[:, j * LANES:(j + 1) * LANES]
            r = blk * cos + pltpu.roll(blk, half, 1) * sa + pltpu.roll(blk, LANES - half, 1) * sb
            k_ref[:, j * LANES:(j + 1) * LANES] = r.astype(BF16)

    def emit_qT():
        scale = ATTN_HEAD_DIM ** -0.5 * math.log2(math.e)
        cosT, sinT = cosT_ref[...] * scale, sinT_ref[...] * scale
        accT = lax.dot_general(wqT_ref[...], h, nt, preferred_element_type=F32)
        for g in range(D_MODEL // ATTN_HEAD_DIM):
            lo = g * ATTN_HEAD_DIM
            x1 = accT[lo:lo + half, :]
            x2 = accT[lo + half:lo + 2 * half, :]
            qT_ref[lo:lo + half, :] = (x1 * cosT - x2 * sinT).astype(BF16)
            qT_ref[lo + half:lo + 2 * half, :] = (x2 * cosT + x1 * sinT).astype(BF16)

    cw = cw_ref[...]
    cb = cb_ref[...]
    KW = SSD_CONV_WIDTH
    n_chunks = SSD_CONV_DIM // CONV_CHUNK
    fillers = [emit_zs, emit_gates_dt, emit_ga, emit_vT, emit_k, emit_qT]
    assert len(fillers) == n_chunks

    def xbc_product(c):
        lo = _OFF_XBC + c * CONV_CHUNK
        return jnp.dot(h_ext, w_ref[:, lo:lo + CONV_CHUNK], preferred_element_type=F32)

    nxt = xbc_product(0)
    for c in range(n_chunks):
        cols = slice(c * CONV_CHUNK, (c + 1) * CONV_CHUNK)
        xe = nxt
        if c + 1 < n_chunks:
            nxt = xbc_product(c + 1)
        fillers[c]()
        slot = c % 2
        cbuf_ref[slot, 0:ROW_HALO, :] = xe[0:ROW_HALO, :] * keep_halo
        cbuf_ref[slot, ROW_HALO:, :] = xe[ROW_HALO:, :]
        conv = cb[:, cols] + cw[KW - 1:KW, cols] * xe[ROW_HALO:, :]
        for j in range(KW - 1):
            shift = KW - 1 - j
            conv = conv + cw[j:j + 1, cols] * cbuf_ref[slot, pl.ds(ROW_HALO - shift, tm), :]
        xc = _silu(conv).astype(BF16)
        for lt in range(CONV_CHUNK // LANES):
            xc_ref[c * (CONV_CHUNK // LANES) + lt] = xc[:, lt * LANES:(lt + 1) * LANES]


def _in_projection(x2d, norm_w, w_all, w_qT, w_vT, w_dtT, conv_w, conv_b, dt_bias, rope, seq, tm):
    t, d = x2d.shape
    n_seq_tiles = seq // tm
    cos_t, sa_t, sb_t, cosT_t, sinT_t = rope
    half = ATTN_HEAD_DIM // 2
    halo_blocks = tm // ROW_HALO
    row = lambda i: (i, 0)
    colblk = lambda i: (0, i)
    tab = lambda i: (i % n_seq_tiles, 0)
    tabT = lambda i: (0, i % n_seq_tiles)
    big = jax.ShapeDtypeStruct((t, D_MODEL), BF16)
    bigT = jax.ShapeDtypeStruct((D_MODEL, t), BF16)
    dtb_row = jnp.pad(dt_bias, (0, LANES - SSD_N_HEADS)).reshape(1, LANES)
    dtb_col = dt_bias.reshape(SSD_N_HEADS, 1)
    return pl.pallas_call(
        functools.partial(_inproj_kernel, tm=tm, tiles_per_seq=n_seq_tiles),
        grid=(t // tm,),
        in_specs=[
            pl.BlockSpec((tm, d), row),
            pl.BlockSpec((ROW_HALO, d), lambda i: (jnp.maximum(i * halo_blocks - 1, 0), 0)),
            _const_spec((1, d)),
            _const_spec(w_all.shape),
            _const_spec(w_qT.shape),
            _const_spec(w_vT.shape),
            _const_spec(w_dtT.shape),
            _const_spec((SSD_CONV_WIDTH, SSD_CONV_DIM)),
            _const_spec((1, SSD_CONV_DIM)),
            _const_spec((1, LANES)),
            _const_spec((SSD_N_HEADS, 1)),
            pl.BlockSpec((tm, LANES), tab),
            pl.BlockSpec((tm, LANES), tab),
            pl.BlockSpec((tm, LANES), tab),
            pl.BlockSpec((half, tm), tabT),
            pl.BlockSpec((half, tm), tabT),
        ],
        out_specs=[
            pl.BlockSpec((tm, D_MODEL), row),
            pl.BlockSpec((D_MODEL, tm), colblk),
            pl.BlockSpec((tm, D_MODEL), row),
            pl.BlockSpec((D_MODEL, tm), colblk),
            pl.BlockSpec((tm, D_MODEL), row),
            pl.BlockSpec((tm, D_MODEL), row),
            pl.BlockSpec((SSD_CONV_DIM // LANES, tm, LANES), lambda i: (0, i, 0)),
            pl.BlockSpec((tm, LANES), row),
            pl.BlockSpec((SSD_N_HEADS, tm), colblk),
        ],
        out_shape=[big, bigT, big, bigT, big, big,
                   jax.ShapeDtypeStruct((SSD_CONV_DIM // LANES, t, LANES), BF16),
                   jax.ShapeDtypeStruct((t, LANES), F32),
                   jax.ShapeDtypeStruct((SSD_N_HEADS, t), F32)],
        scratch_shapes=[pltpu.VMEM((2, ROW_HALO + tm, CONV_CHUNK), F32)],
        compiler_params=pltpu.CompilerParams(
            dimension_semantics=("arbitrary",), vmem_limit_bytes=VMEM_LIMIT_BYTES),
        name="norm_in_projection",
    )(x2d, x2d, norm_w, w_all, w_qT, w_vT, w_dtT, conv_w, conv_b.reshape(1, -1), dtb_row, dtb_col,
      cos_t, sa_t, sb_t, cosT_t, sinT_t)


def _ssd_kernel(zs_ref, xc_ref, dt_ref, dtT_ref, alog_row_ref, alog_col_ref, dexp_ref, nw_ref, expand_ref,
                y_ref, state_ref, *, chunks_per_step):
    L = SSD_CHUNK
    P = SSD_HEAD_DIM
    N = SSD_D_STATE
    heads_per_group = SSD_N_HEADS // SSD_N_GROUPS

    @pl.when(pl.program_id(1) == 0)
    def _():
        state_ref[...] = jnp.zeros_like(state_ref)

    neg_a_row = -jnp.exp(alog_row_ref[...])
    neg_a_col = -jnp.exp(alog_col_ref[...])
    ri = lax.broadcasted_iota(jnp.int32, (L, L), 0)
    ci = lax.broadcasted_iota(jnp.int32, (L, L), 1)
    causal = ri >= ci
    tri = jnp.where(causal, 1.0, 0.0).astype(BF16)
    triT = jnp.where(ri <= ci, 1.0, 0.0).astype(BF16)
    lane = lax.broadcasted_iota(jnp.int32, (L, LANES), 1)
    first_half = lane < P
    gw = D_MODEL // SSD_N_GROUPS

    def one_chunk(step, carry):
        rows = pl.ds(pl.multiple_of(step * L, L), L)
        x_slabs = [xc_ref[lt, rows, :] for lt in range(D_MODEL // LANES)]
        xs = jnp.concatenate(x_slabs, axis=1).astype(F32)

        dt = dt_ref[rows, :]
        dtT = dtT_ref[:, rows]
        dA = dt * neg_a_row
        dAT = dtT * neg_a_col
        a_cs = sum(jnp.dot(tri, part, preferred_element_type=F32) for part in _split3(dA))
        a_csT = sum(jnp.dot(part, triT, preferred_element_type=F32) for part in _split3(dAT))
        a_last = a_cs[L - 1:L, :]
        w_state = dt * jnp.exp(a_last - a_cs)

        expand = expand_ref[...]
        w_state_x = sum(jnp.dot(part, expand, preferred_element_type=F32) for part in _split2(w_state))
        a_tail = jnp.broadcast_to(a_last, (SUBLANES, LANES))
        a_last_x = sum(jnp.dot(part, expand, preferred_element_type=F32) for part in _split3(a_tail))[0:1, :]
        decay_tot = jnp.exp(a_last_x)
        assert gw == heads_per_group * P
        for g in range(SSD_N_GROUPS):
            gcols = slice(g * gw, (g + 1) * gw)
            b_gb = xc_ref[D_MODEL // LANES + g, rows, :]
            c_gb = xc_ref[D_MODEL // LANES + SSD_N_GROUPS + g, rows, :]
            c_g = c_gb.astype(F32)
            cb = lax.dot_general(c_gb, b_gb, (((1,), (1,)), ((), ())), preferred_element_type=F32)
            y_parts = []
            for pair in range(heads_per_group // 2):
                col0 = (g * heads_per_group + 2 * pair) * P
                pair_idx = g * (heads_per_group // 2) + pair
                x_pair = x_slabs[pair_idx]
                prev_pair = state_ref[pair_idx].astype(BF16)
                outs = []
                for k in range(2):
                    hd = g * heads_per_group + 2 * pair + k
                    a_col = jnp.broadcast_to(a_cs[:, hd:hd + 1], (L, L))
                    seg = a_col - a_csT[hd:hd + 1, :]
                    decay = jnp.exp(jnp.where(causal, seg, -jnp.inf))
                    m_h = (cb * decay * dtT[hd:hd + 1, :]).astype(BF16)
                    c_h = (c_g * jnp.exp(a_col)).astype(BF16)
                    outs.append(jnp.dot(m_h, x_pair, preferred_element_type=F32)
                                + jnp.dot(c_h, prev_pair, preferred_element_type=F32))
                y_parts.append(jnp.where(first_half, outs[0], outs[1]))
            xs_g = xs[:, gcols]
            xd_g = (xs_g * w_state_x[:, gcols]).astype(BF16)
            new_state = lax.dot_general(b_gb, xd_g, (((0,), (0,)), ((), ())), preferred_element_type=F32)
            for q in range(heads_per_group // 2):
                pair_idx = g * (heads_per_group // 2) + q
                lanes = slice(pair_idx * LANES, (pair_idx + 1) * LANES)
                state_ref[pair_idx] = state_ref[pair_idx] * decay_tot[:, lanes] + new_state[:, q * LANES:(q + 1) * LANES]
            y_g = jnp.concatenate(y_parts, axis=1) + xs_g * dexp_ref[:, gcols]
            yg = y_g * zs_ref[rows, gcols].astype(F32)
            y_ref[rows, gcols] = _rms(yg, nw_ref[:, gcols], NORM_EPS).astype(BF16)
        return carry

    lax.fori_loop(0, chunks_per_step, one_chunk, 0, unroll=True)


def _ssd_branch(zs, xc, dt, dtT, a_log, d_skip, norm_w, batch, seq):
    L = SSD_CHUNK
    nc = seq // L
    t = batch * seq
    alog_row = jnp.pad(a_log, (0, LANES - SSD_N_HEADS)).reshape(1, LANES)
    alog_col = a_log.reshape(SSD_N_HEADS, 1)
    dexp = jnp.repeat(d_skip, SSD_HEAD_DIM).reshape(1, D_MODEL)
    expand = (jnp.arange(LANES)[:, None] == (jnp.arange(D_MODEL)[None, :] // SSD_HEAD_DIM)).astype(BF16)
    cps = _tile(nc, 8)
    ns = nc // cps
    rows = cps * L
    row = lambda b, c: (b * ns + c, 0)
    return pl.pallas_call(
        functools.partial(_ssd_kernel, chunks_per_step=cps),
        grid=(batch, ns),
        in_specs=[
            pl.BlockSpec((rows, D_MODEL), row),
            pl.BlockSpec((SSD_CONV_DIM // LANES, rows, LANES), lambda b, c: (0, b * ns + c, 0)),
            pl.BlockSpec((rows, LANES), row),
            pl.BlockSpec((SSD_N_HEADS, rows), lambda b, c: (0, b * ns + c)),
            _const_spec((1, LANES)),
            _const_spec((SSD_N_HEADS, 1)),
            _const_spec((1, D_MODEL)),
            _const_spec((1, D_MODEL)),
            _const_spec((LANES, D_MODEL)),
        ],
        out_specs=pl.BlockSpec((rows, D_MODEL), row),
        out_shape=jax.ShapeDtypeStruct((t, D_MODEL), BF16),
        scratch_shapes=[pltpu.VMEM((D_MODEL // LANES, SSD_D_STATE, LANES), F32)],
        compiler_params=pltpu.CompilerParams(
            dimension_semantics=("arbitrary", "arbitrary"), vmem_limit_bytes=VMEM_LIMIT_BYTES),
        name="ssd_branch",
    )(zs, xc, dt, dtT, alog_row, alog_col, dexp, norm_w.reshape(1, -1), expand)


def _attn_kernel(qT_ref, k_ref, vT_ref, lq1_ref, lk1_ref, lq2_ref, lk2_ref, swc_ref, o_ref,
                 s_ref, p_ref, alpha_ref, m_ref, acc_ref, qm_ref, vext_ref, *, seq, tq):
    d = ATTN_HEAD_DIM
    dv = 2 * d
    tk = tq
    nq = seq // tq
    neg_inf = -jnp.inf

    key_i = lax.broadcasted_iota(jnp.int32, (tk, LANES), 0)
    qry_i = lax.broadcasted_iota(jnp.int32, (tk, LANES), 1)
    n_lt = tq // LANES
    feat = lax.broadcasted_iota(jnp.int32, (dv, LANES), 0)
    ones_rows = jnp.ones((ONES_ROWS, LANES), BF16)
    for ct in range(seq // LANES):
        q_c = qT_ref[:, ct * LANES:(ct + 1) * LANES]
        zero = jnp.zeros_like(q_c)
        qm_ref[0, ct] = jnp.where(feat < d, q_c, zero)
        qm_ref[1, ct] = jnp.where(feat >= d, q_c, zero)
        vext_ref[ct, 0:dv, :] = vT_ref[:, ct * LANES:(ct + 1) * LANES]
        vext_ref[ct, dv:, :] = ones_rows
    lam = (jnp.exp(jnp.sum(lq1_ref[...] * lk1_ref[...], axis=-1, keepdims=True))
           - jnp.exp(jnp.sum(lq2_ref[...] * lk2_ref[...], axis=-1, keepdims=True)) + LAMBDA_INIT)
    sw_col = swc_ref[...]

    def lane_tiles(ref, lead, first, count):
        return jnp.concatenate([ref[lead + (first + c,)] for c in range(count)], axis=1)

    steps = [(i, j) for i in range(1, nq) for j in range(i)] + [(i, i) for i in range(nq)]
    visited = set()
    first_visit = []
    for i, _ in steps:
        first_visit.append(i not in visited)
        visited.add(i)

    def stage_scores(t):
        i, j = steps[t]
        kb = k_ref[j * tk:(j + 1) * tk, :]
        for mp in range(2):
            s = jnp.dot(kb, lane_tiles(qm_ref, (mp,), i * n_lt, n_lt), preferred_element_type=F32)
            for c in range(n_lt):
                s_ref[t % 2, mp, c] = s[:, c * LANES:(c + 1) * LANES]

    def stage_softmax(t):
        i, j = steps[t]
        slot = t % 2
        for mp in range(2):
            for c in range(n_lt):

                def scores():
                    ss = s_ref[slot, mp, c]
                    if i == j:
                        ss = jnp.where(key_i <= qry_i + c * LANES, ss, neg_inf)
                    return ss

                m_new = jnp.max(scores(), axis=0, keepdims=True)
                if not first_visit[t]:
                    m_prev = m_ref[i, mp, c]
                    m_new = jnp.maximum(m_prev, m_new)
                    alpha_ref[slot, mp, c] = jnp.exp2(m_prev - m_new)
                p_ref[slot, mp, c] = jnp.exp2(scores() - m_new).astype(BF16)
                m_ref[i, mp, c] = m_new

    def stage_values(t):
        i, j = steps[t]
        slot = t % 2
        v_ext = lane_tiles(vext_ref, (), j * n_lt, n_lt)
        for mp in range(2):
            pv = jnp.dot(v_ext, lane_tiles(p_ref, (slot, mp), 0, n_lt), preferred_element_type=F32)
            for c in range(n_lt):
                pv_c = pv[:, c * LANES:(c + 1) * LANES]
                if first_visit[t]:
                    acc_ref[i, mp, c] = pv_c
                else:
                    acc_ref[i, mp, c] = alpha_ref[slot, mp, c] * acc_ref[i, mp, c] + pv_c

    n = len(steps)
    stage_scores(0)
    if n > 1:
        stage_scores(1)
    for t in range(n + 2):
        if t >= 2:
            stage_values(t - 2)
        if t < n:
            stage_softmax(t)
        if t + 2 < n:
            stage_scores(t + 2)

    for i in range(nq):
        for c in range(n_lt):
            inv_l = [1.0 / acc_ref[i, mp, c, dv:dv + 1, :] for mp in range(2)]
            oT = acc_ref[i, 0, c, 0:dv, :] * inv_l[0] - lam * (acc_ref[i, 1, c, 0:dv, :] * inv_l[1])
            ms = jnp.mean(oT * oT, axis=0, keepdims=True)
            y = oT * lax.rsqrt(ms + SUBLN_EPS) * sw_col * (1.0 - LAMBDA_INIT)
            o_ref[i * tq + c * LANES:i * tq + (c + 1) * LANES, :] = y.T.astype(BF16)


def _diff_attention(qT, k, vT, lq1, lk1, lq2, lk2, subln_w, batch, seq, tq):
    t = batch * seq
    vec = lambda a: a.reshape(1, -1)
    rows_blk = pl.BlockSpec((seq, LANES), lambda b, h: (b, h))
    cols_blk = pl.BlockSpec((LANES, seq), lambda b, h: (h, b))
    return pl.pallas_call(
        functools.partial(_attn_kernel, seq=seq, tq=tq),
        grid=(batch, ATTN_N_HEADS),
        in_specs=[
            cols_blk, rows_blk, cols_blk,
            _const_spec((1, ATTN_HEAD_DIM)),
            _const_spec((1, ATTN_HEAD_DIM)),
            _const_spec((1, ATTN_HEAD_DIM)),
            _const_spec((1, ATTN_HEAD_DIM)),
            _const_spec((2 * ATTN_HEAD_DIM, 1)),
        ],
        out_specs=rows_blk,
        out_shape=jax.ShapeDtypeStruct((t, D_MODEL), BF16),
        scratch_shapes=[
            pltpu.VMEM((2, 2, tq // LANES, tq, LANES), F32),
            pltpu.VMEM((2, 2, tq // LANES, tq, LANES), BF16),
            pltpu.VMEM((2, 2, tq // LANES, 1, LANES), F32),
            pltpu.VMEM((seq // tq, 2, tq // LANES, 1, LANES), F32),
            pltpu.VMEM((seq // tq, 2, tq // LANES, LANES + ONES_ROWS, LANES), F32),
            pltpu.VMEM((2, seq // LANES, LANES, LANES), BF16),
            pltpu.VMEM((seq // LANES, LANES + ONES_ROWS, LANES), BF16),
        ],
        compiler_params=pltpu.CompilerParams(
            dimension_semantics=("arbitrary", "arbitrary"),
            vmem_limit_bytes=VMEM_LIMIT_BYTES),
        name="diff_attention",
    )(qT, k, vT, vec(lq1), vec(lk1), vec(lq2), vec(lk2), subln_w.reshape(-1, 1))


def _merge_kernel(x_ref, ys_ref, ya_ref, gs_ref, ga_ref, wbs_ref, wba_ref, wo_ref, o_ref):
    a = jnp.dot(ys_ref[...], wbs_ref[...], preferred_element_type=F32)
    b = jnp.dot(ya_ref[...], wba_ref[...], preferred_element_type=F32)
    merged = (jax.nn.sigmoid(gs_ref[...].astype(F32)) * a
              + jax.nn.sigmoid(ga_ref[...].astype(F32)) * b)
    o_ref[...] = x_ref[...] + jnp.dot(merged.astype(BF16), wo_ref[...], preferred_element_type=F32)


def _merge(x2d, ys, ya, gs, ga, wbs, wba, wo, tm):
    t, d = x2d.shape
    row = lambda i: (i, 0)
    tile = pl.BlockSpec((tm, d), row)
    return pl.pallas_call(
        _merge_kernel,
        grid=(t // tm,),
        in_specs=[tile, tile, tile, tile, tile,
                  _const_spec((d, d)), _const_spec((d, d)), _const_spec((d, d))],
        out_specs=tile,
        out_shape=jax.ShapeDtypeStruct((t, d), F32),
        compiler_params=pltpu.CompilerParams(
            dimension_semantics=("arbitrary",), vmem_limit_bytes=VMEM_LIMIT_BYTES),
        name="gated_merge_out_projection",
    )(x2d, ys, ya, gs, ga, wbs, wba, wo)


def _ffn_kernel(x_ref, halo_ref, nw_ref, wup_ref, cw_ref, cb_ref, wd_ref, fw_ref, o_ref, gbuf_ref, act_ref,
                *, tm, tiles_per_seq, chunk):
    i = pl.program_id(0)
    x1 = x_ref[...]
    x_ext = jnp.concatenate([halo_ref[...], x1], axis=0)
    h_ext = _rms(x_ext, nw_ref[...], NORM_EPS).astype(BF16)
    h2 = h_ext[FFN_HALO:, :]
    keep_halo = jnp.where(i % tiles_per_seq == 0, 0.0, 1.0)
    cw = cw_ref[...]
    cb = cb_ref[...]
    KW = FFN_CONV_WIDTH
    n_chunks = D_FF // chunk

    def up(c):
        lo = c * chunk
        gate_ext = jnp.dot(h_ext, wup_ref[:, lo:lo + chunk], preferred_element_type=F32)
        val = jnp.dot(h2, wup_ref[:, D_FF + lo:D_FF + lo + chunk], preferred_element_type=F32)
        return gate_ext, val

    acc = jnp.zeros((tm, D_MODEL), F32)
    nxt = up(0)
    for c in range(n_chunks):
        lo = c * chunk
        gate_ext, val = nxt
        if c + 1 < n_chunks:
            nxt = up(c + 1)
        slot = c % 2
        gbuf_ref[slot, 0:FFN_HALO, :] = gate_ext[0:FFN_HALO, :] * keep_halo
        gbuf_ref[slot, FFN_HALO:, :] = gate_ext[FFN_HALO:, :]
        conv = cb[:, lo:lo + chunk] + cw[KW - 1:KW, lo:lo + chunk] * gate_ext[FFN_HALO:, :]
        for j in range(KW - 1):
            shift = KW - 1 - j
            conv = conv + cw[j:j + 1, lo:lo + chunk] * gbuf_ref[slot, pl.ds(FFN_HALO - shift, tm), :]
        act_ref[:, lo:lo + chunk] = (_silu(conv) * val).astype(BF16)
        if (c + 1) % DOWN_GROUP == 0 or c + 1 == n_chunks:
            k0 = (c // DOWN_GROUP) * DOWN_GROUP * chunk
            acc = acc + jnp.dot(act_ref[:, k0:lo + chunk], wd_ref[k0:lo + chunk, :], preferred_element_type=F32)
    o_ref[...] = _rms(x1 + acc, fw_ref[...], NORM_EPS)


def _conv_ffn(x1, norm_w, w_up, conv_w, conv_b, w_down, final_w, seq, tm, chunk):
    t, d = x1.shape
    tiles_per_seq = seq // tm
    halo_blocks = tm // FFN_HALO
    return pl.pallas_call(
        functools.partial(_ffn_kernel, tm=tm, tiles_per_seq=tiles_per_seq, chunk=chunk),
        grid=(t // tm,),
        in_specs=[
            pl.BlockSpec((tm, d), lambda i: (i, 0)),
            pl.BlockSpec((FFN_HALO, d), lambda i: (jnp.maximum(i * halo_blocks - 1, 0), 0)),
            _const_spec((1, d)),
            _const_spec(w_up.shape),
            _const_spec(conv_w.shape),
            _const_spec((1, D_FF)),
            _const_spec(w_down.shape),
            _const_spec((1, d)),
        ],
        out_specs=pl.BlockSpec((tm, d), lambda i: (i, 0)),
        out_shape=jax.ShapeDtypeStruct((t, d), F32),
        scratch_shapes=[pltpu.VMEM((2, FFN_HALO + tm, chunk), F32), pltpu.VMEM((tm, D_FF), BF16)],
        compiler_params=pltpu.CompilerParams(
            dimension_semantics=("arbitrary",), vmem_limit_bytes=VMEM_LIMIT_BYTES),
        name="conv_ffn_final_norm",
    )(x1, x1, norm_w.reshape(1, -1), w_up, conv_w, conv_b.reshape(1, -1), w_down, final_w.reshape(1, -1))


def _rope_tables(seq):
    half = ATTN_HEAD_DIM // 2
    inv = 1.0 / (ROPE_THETA ** (jnp.arange(0, ATTN_HEAD_DIM, 2, dtype=F32) / ATTN_HEAD_DIM))
    ang = jnp.arange(seq, dtype=F32)[:, None] * inv[None, :]
    lane = jnp.arange(LANES)
    cos = jnp.cos(ang)[:, lane % half]
    sin = jnp.sin(ang)[:, lane % half]
    second = (lane % ATTN_HEAD_DIM) >= half
    sa = jnp.where(second[None, :], sin, 0.0)
    sb = jnp.where(second[None, :], 0.0, -sin)
    return cos, sa, sb, jnp.cos(ang).T, jnp.sin(ang).T


def _tile(n, pref):
    while n % pref:
        pref //= 2
    return pref


def kernel(x, norm_mix_w, w_in, ssd_conv_w, ssd_conv_b, ssd_dt_bias, ssd_a_log, ssd_d_skip, ssd_norm_w, lambda_q1, lambda_k1, lambda_q2, lambda_k2, subln_w, w_branch_ssd, w_branch_attn, w_out, norm_ffn_w, w_up, ffn_conv_w, ffn_conv_b, w_down, final_norm_w):
    batch, seq, d = x.shape
    assert d == D_MODEL and seq % SSD_CHUNK == 0
    t = batch * seq
    x2d = x.reshape(t, d)
    li = 0

    w = w_in[li]
    o_z, o_xbc = 0, D_MODEL
    o_dt = o_xbc + SSD_CONV_DIM
    o_q = o_dt + SSD_N_HEADS
    cols = lambda lo, n: w[:, lo:lo + n]
    w_dt = cols(o_dt, SSD_N_HEADS)
    w_all = jnp.concatenate([
        cols(o_z, D_MODEL), cols(o_q + D_MODEL, D_MODEL),
        cols(o_q + 3 * D_MODEL, D_MODEL), cols(o_q + 4 * D_MODEL, D_MODEL), cols(o_xbc, SSD_CONV_DIM),
        jnp.pad(w_dt, ((0, 0), (0, LANES - SSD_N_HEADS)))], axis=1).astype(BF16)
    assert w_all.shape[1] == _IN_COLS_PADDED
    w_dtT = w_dt.T.astype(BF16)
    w_qT = cols(o_q, D_MODEL).T.astype(BF16)
    w_vT = cols(o_q + 2 * D_MODEL, D_MODEL).T.astype(BF16)

    tm = _tile(seq, 512)
    zs, qT, k, vT, gs, ga, xc, dt, dtT = _in_projection(
        x2d, norm_mix_w[li].reshape(1, -1), w_all, w_qT, w_vT, w_dtT, ssd_conv_w[li], ssd_conv_b[li],
        ssd_dt_bias[li], _rope_tables(seq), seq, tm)

    y_ssd = _ssd_branch(zs, xc, dt, dtT, ssd_a_log[li], ssd_d_skip[li], ssd_norm_w[li], batch, seq)
    tq = _tile(seq, 256)
    y_attn = _diff_attention(qT, k, vT, lambda_q1[li], lambda_k1[li], lambda_q2[li], lambda_k2[li],
                             subln_w[li], batch, seq, tq)
    x1 = _merge(x2d, y_ssd, y_attn, gs, ga, w_branch_ssd[li].astype(BF16), w_branch_attn[li].astype(BF16),
                w_out[li].astype(BF16), _tile(seq, 1024))
    out = _conv_ffn(x1, norm_ffn_w[li], w_up[li].astype(BF16), ffn_conv_w[li], ffn_conv_b[li],
                    w_down[li].astype(BF16), final_norm_w, seq, _tile(seq, 1024), 256)
    return out.reshape(batch, seq, d)
```

```python
import functools
import math

import jax
import jax.numpy as jnp
from jax import lax
from jax.experimental import pallas as pl
from jax.experimental.pallas import tpu as pltpu

F32 = jnp.float32
BF16 = jnp.bfloat16

D_MODEL = 1024
SSD_HEAD_DIM = 64
SSD_N_HEADS = 16
SSD_N_GROUPS = 2
SSD_D_STATE = 128
SSD_CONV_WIDTH = 4
SSD_CHUNK = 128
SSD_CONV_DIM = D_MODEL + 2 * SSD_N_GROUPS * SSD_D_STATE
ATTN_HEAD_DIM = 64
ATTN_N_HEADS = 8
ROPE_THETA = 10000.0
D_FF = 2816
FFN_CONV_WIDTH = 3
NORM_EPS = 1e-6
SUBLN_EPS = 1e-5
LAMBDA_INIT = 0.8 - 0.6 * math.exp(-0.3 * 0)

LANES = 128
SUBLANES = 8
ONES_ROWS = 16
ROW_HALO = 16
FFN_HALO = ROW_HALO
CONV_CHUNK = 256
DOWN_GROUP = 4
VMEM_LIMIT_BYTES = 56 * 1024 * 1024

_OFF_Z, _OFF_K, _OFF_GS, _OFF_GA = (i * D_MODEL for i in range(4))
_OFF_XBC = 4 * D_MODEL
_OFF_DT = _OFF_XBC + SSD_CONV_DIM
_IN_COLS_PADDED = _OFF_DT + LANES


def _const_spec(shape):
    nd = len(shape)
    return pl.BlockSpec(shape, lambda *_: (0,) * nd, pipeline_mode=pl.Buffered(1))


def _rms(x, w, eps):
    return x * lax.rsqrt(jnp.mean(x * x, axis=-1, keepdims=True) + eps) * w


def _split2(x):
    hi = x.astype(BF16)
    lo = (x - hi.astype(F32)).astype(BF16)
    return hi, lo


def _split3(x):
    hi = x.astype(BF16)
    r = x - hi.astype(F32)
    mid = r.astype(BF16)
    lo = (r - mid.astype(F32)).astype(BF16)
    return hi, mid, lo


def _softplus(x):
    return jnp.maximum(x, 0.0) + jnp.log1p(jnp.exp(-jnp.abs(x)))


def _silu(x):
    return x * jax.nn.sigmoid(x)


def _inproj_kernel(x_ref, halo_ref, nw_ref, w_ref, wqT_ref, wvT_ref, wdt_ref, cw_ref, cb_ref, dtb_row_ref,
                   dtb_col_ref, cos_ref, sa_ref, sb_ref, cosT_ref, sinT_ref,
                   zs_ref, qT_ref, k_ref, vT_ref, gs_ref, ga_ref, xc_ref, dt_ref, dtT_ref, cbuf_ref,
                   *, tm, tiles_per_seq):
    i = pl.program_id(0)
    x_ext = jnp.concatenate([halo_ref[...], x_ref[...]], axis=0)
    h_ext = _rms(x_ext, nw_ref[...], NORM_EPS).astype(BF16)
    h = h_ext[ROW_HALO:, :]
    keep_halo = jnp.where(i % tiles_per_seq == 0, 0.0, 1.0)
    nt = (((1,), (1,)), ((), ()))

    def mm(lo, width):
        return jnp.dot(h, w_ref[:, lo:lo + width], preferred_element_type=F32)

    half = ATTN_HEAD_DIM // 2

    def emit_zs():
        zs_ref[...] = _silu(mm(_OFF_Z, D_MODEL)).astype(BF16)

    def emit_gates_dt():
        gs_ref[...] = mm(_OFF_GS, D_MODEL).astype(BF16)
        dt_ref[...] = _softplus(mm(_OFF_DT, LANES) + dtb_row_ref[...])
        dtT_ref[...] = _softplus(lax.dot_general(wdt_ref[...], h, nt, preferred_element_type=F32)
                                 + dtb_col_ref[...])

    def emit_ga():
        ga_ref[...] = mm(_OFF_GA, D_MODEL).astype(BF16)

    def emit_vT():
        vT_ref[...] = lax.dot_general(wvT_ref[...], h, nt, preferred_element_type=F32).astype(BF16)

    def emit_k():
        cos, sa, sb = cos_ref[...], sa_ref[...], sb_ref[...]
        acc = mm(_OFF_K, D_MODEL)
        for j in range(D_MODEL // LANES):
            blk = acc[:, j * LANES:(j + 1) * LANES]
            r = blk * cos + pltpu.roll(blk, half, 1) * sa + pltpu.roll(blk, LANES - half, 1) * sb
            k_ref[:, j * LANES:(j + 1) * LANES] = r.astype(BF16)

    def emit_qT():
        scale = ATTN_HEAD_DIM ** -0.5 * math.log2(math.e)
        cosT, sinT = cosT_ref[...] * scale, sinT_ref[...] * scale
        accT = lax.dot_general(wqT_ref[...], h, nt, preferred_element_type=F32)
        for g in range(D_MODEL // ATTN_HEAD_DIM):
            lo = g * ATTN_HEAD_DIM
            x1 = accT[lo:lo + half, :]
            x2 = accT[lo + half:lo + 2 * half, :]
            qT_ref[lo:lo + half, :] = (x1 * cosT - x2 * sinT).astype(BF16)
            qT_ref[lo + half:lo + 2 * half, :] = (x2 * cosT + x1 * sinT).astype(BF16)

    cw = cw_ref[...]
    cb = cb_ref[...]
    KW = SSD_CONV_WIDTH
    n_chunks = SSD_CONV_DIM // CONV_CHUNK
    fillers = [emit_zs, emit_gates_dt, emit_ga, emit_vT, emit_k, emit_qT]
    assert len(fillers) == n_chunks

    def xbc_product(c):
        lo = _OFF_XBC + c * CONV_CHUNK
        return jnp.dot(h_ext, w_ref[:, lo:lo + CONV_CHUNK], preferred_element_type=F32)

    nxt = xbc_product(0)
    for c in range(n_chunks):
        cols = slice(c * CONV_CHUNK, (c + 1) * CONV_CHUNK)
        xe = nxt
        if c + 1 < n_chunks:
            nxt = xbc_product(c + 1)
        fillers[c]()
        slot = c % 2
        cbuf_ref[slot, 0:ROW_HALO, :] = xe[0:ROW_HALO, :] * keep_halo
        cbuf_ref[slot, ROW_HALO:, :] = xe[ROW_HALO:, :]
        conv = cb[:, cols] + cw[KW - 1:KW, cols] * xe[ROW_HALO:, :]
        for j in range(KW - 1):
            shift = KW - 1 - j
            conv = conv + cw[j:j + 1, cols] * cbuf_ref[slot, pl.ds(ROW_HALO - shift, tm), :]
        xc = _silu(conv).astype(BF16)
        for lt in range(CONV_CHUNK // LANES):
            xc_ref[c * (CONV_CHUNK // LANES) + lt] = xc[:, lt * LANES:(lt + 1) * LANES]


def _in_projection(x2d, norm_w, w_all, w_qT, w_vT, w_dtT, conv_w, conv_b, dt_bias, rope, seq, tm):
    t, d = x2d.shape
    n_seq_tiles = seq // tm
    cos_t, sa_t, sb_t, cosT_t, sinT_t = rope
    half = ATTN_HEAD_DIM // 2
    halo_blocks = tm // ROW_HALO
    row = lambda i: (i, 0)
    colblk = lambda i: (0, i)
    tab = lambda i: (i % n_seq_tiles, 0)
    tabT = lambda i: (0, i % n_seq_tiles)
    big = jax.ShapeDtypeStruct((t, D_MODEL), BF16)
    bigT = jax.ShapeDtypeStruct((D_MODEL, t), BF16)
    dtb_row = jnp.pad(dt_bias, (0, LANES - SSD_N_HEADS)).reshape(1, LANES)
    dtb_col = dt_bias.reshape(SSD_N_HEADS, 1)
    return pl.pallas_call(
        functools.partial(_inproj_kernel, tm=tm, tiles_per_seq=n_seq_tiles),
        grid=(t // tm,),
        in_specs=[
            pl.BlockSpec((tm, d), row),
            pl.BlockSpec((ROW_HALO, d), lambda i: (jnp.maximum(i * halo_blocks - 1, 0), 0)),
            _const_spec((1, d)),
            _const_spec(w_all.shape),
            _const_spec(w_qT.shape),
            _const_spec(w_vT.shape),
            _const_spec(w_dtT.shape),
            _const_spec((SSD_CONV_WIDTH, SSD_CONV_DIM)),
            _const_spec((1, SSD_CONV_DIM)),
            _const_spec((1, LANES)),
            _const_spec((SSD_N_HEADS, 1)),
            pl.BlockSpec((tm, LANES), tab),
            pl.BlockSpec((tm, LANES), tab),
            pl.BlockSpec((tm, LANES), tab),
            pl.BlockSpec((half, tm), tabT),
            pl.BlockSpec((half, tm), tabT),
        ],
        out_specs=[
            pl.BlockSpec((tm, D_MODEL), row),
            pl.BlockSpec((D_MODEL, tm), colblk),
            pl.BlockSpec((tm, D_MODEL), row),
            pl.BlockSpec((D_MODEL, tm), colblk),
            pl.BlockSpec((tm, D_MODEL), row),
            pl.BlockSpec((tm, D_MODEL), row),
            pl.BlockSpec((SSD_CONV_DIM // LANES, tm, LANES), lambda i: (0, i, 0)),
            pl.BlockSpec((tm, LANES), row),
            pl.BlockSpec((SSD_N_HEADS, tm), colblk),
        ],
        out_shape=[big, bigT, big, bigT, big, big,
                   jax.ShapeDtypeStruct((SSD_CONV_DIM // LANES, t, LANES), BF16),
                   jax.ShapeDtypeStruct((t, LANES), F32),
                   jax.ShapeDtypeStruct((SSD_N_HEADS, t), F32)],
        scratch_shapes=[pltpu.VMEM((2, ROW_HALO + tm, CONV_CHUNK), F32)],
        compiler_params=pltpu.CompilerParams(
            dimension_semantics=("arbitrary",), vmem_limit_bytes=VMEM_LIMIT_BYTES),
        name="norm_in_projection",
    )(x2d, x2d, norm_w, w_all, w_qT, w_vT, w_dtT, conv_w, conv_b.reshape(1, -1), dtb_row, dtb_col,
      cos_t, sa_t, sb_t, cosT_t, sinT_t)


def _ssd_kernel(zs_ref, xc_ref, dt_ref, dtT_ref, alog_row_ref, alog_col_ref, dexp_ref, nw_ref, expand_ref,
                y_ref, state_ref, *, chunks_per_step):
    L = SSD_CHUNK
    P = SSD_HEAD_DIM
    N = SSD_D_STATE
    heads_per_group = SSD_N_HEADS // SSD_N_GROUPS

    @pl.when(pl.program_id(1) == 0)
    def _():
        state_ref[...] = jnp.zeros_like(state_ref)

    neg_a_row = -jnp.exp(alog_row_ref[...])
    neg_a_col = -jnp.exp(alog_col_ref[...])
    ri = lax.broadcasted_iota(jnp.int32, (L, L), 0)
    ci = lax.broadcasted_iota(jnp.int32, (L, L), 1)
    causal = ri >= ci
    tri = jnp.where(causal, 1.0, 0.0).astype(BF16)
    triT = jnp.where(ri <= ci, 1.0, 0.0).astype(BF16)
    lane = lax.broadcasted_iota(jnp.int32, (L, LANES), 1)
    first_half = lane < P
    gw = D_MODEL // SSD_N_GROUPS

    def one_chunk(step, carry):
        rows = pl.ds(pl.multiple_of(step * L, L), L)
        x_slabs = [xc_ref[lt, rows, :] for lt in range(D_MODEL // LANES)]
        xs = jnp.concatenate(x_slabs, axis=1).astype(F32)

        dt = dt_ref[rows, :]
        dtT = dtT_ref[:, rows]
        dA = dt * neg_a_row
        dAT = dtT * neg_a_col
        a_cs = sum(jnp.dot(tri, part, preferred_element_type=F32) for part in _split3(dA))
        a_csT = sum(jnp.dot(part, triT, preferred_element_type=F32) for part in _split3(dAT))
        a_last = a_cs[L - 1:L, :]
        w_state = dt * jnp.exp(a_last - a_cs)

        expand = expand_ref[...]
        w_state_x = sum(jnp.dot(part, expand, preferred_element_type=F32) for part in _split2(w_state))
        a_tail = jnp.broadcast_to(a_last, (SUBLANES, LANES))
        a_last_x = sum(jnp.dot(part, expand, preferred_element_type=F32) for part in _split3(a_tail))[0:1, :]
        decay_tot = jnp.exp(a_last_x)
        assert gw == heads_per_group * P
        for g in range(SSD_N_GROUPS):
            gcols = slice(g * gw, (g + 1) * gw)
            b_gb = xc_ref[D_MODEL // LANES + g, rows, :]
            c_gb = xc_ref[D_MODEL // LANES + SSD_N_GROUPS + g, rows, :]
            c_g = c_gb.astype(F32)
            cb = lax.dot_general(c_gb, b_gb, (((1,), (1,)), ((), ())), preferred_element_type=F32)
            y_parts = []
            for pair in range(heads_per_group // 2):
                col0 = (g * heads_per_group + 2 * pair) * P
                pair_idx = g * (heads_per_group // 2) + pair
                x_pair = x_slabs[pair_idx]
                prev_pair = state_ref[pair_idx].astype(BF16)
                outs = []
                for k in range(2):
                    hd = g * heads_per_group + 2 * pair + k
                    a_col = jnp.broadcast_to(a_cs[:, hd:hd + 1], (L, L))
                    seg = a_col - a_csT[hd:hd + 1, :]
                    decay = jnp.exp(jnp.where(causal, seg, -jnp.inf))
                    m_h = (cb * decay * dtT[hd:hd + 1, :]).astype(BF16)
                    c_h = (c_g * jnp.exp(a_col)).astype(BF16)
                    outs.append(jnp.dot(m_h, x_pair, preferred_element_type=F32)
                                + jnp.dot(c_h, prev_pair, preferred_element_type=F32))
                y_parts.append(jnp.where(first_half, outs[0], outs[1]))
            xs_g = xs[:, gcols]
            xd_g = (xs_g * w_state_x[:, gcols]).astype(BF16)
            new_state = lax.dot_general(b_gb, xd_g, (((0,), (0,)), ((), ())), preferred_element_type=F32)
            for q in range(heads_per_group // 2):
                pair_idx = g * (heads_per_group // 2) + q
                lanes = slice(pair_idx * LANES, (pair_idx + 1) * LANES)
                state_ref[pair_idx] = state_ref[pair_idx] * decay_tot[:, lanes] + new_state[:, q * LANES:(q + 1) * LANES]
            y_g = jnp.concatenate(y_parts, axis=1) + xs_g * dexp_ref[:, gcols]
            yg = y_g * zs_ref[rows, gcols].astype(F32)
            y_ref[rows, gcols] = _rms(yg, nw_ref[:, gcols], NORM_EPS).astype(BF16)
        return carry

    lax.fori_loop(0, chunks_per_step, one_chunk, 0, unroll=True)


def _ssd_branch(zs, xc, dt, dtT, a_log, d_skip, norm_w, batch, seq):
    L = SSD_CHUNK
    nc = seq // L
    t = batch * seq
    alog_row = jnp.pad(a_log, (0, LANES - SSD_N_HEADS)).reshape(1, LANES)
    alog_col = a_log.reshape(SSD_N_HEADS, 1)
    dexp = jnp.repeat(d_skip, SSD_HEAD_DIM).reshape(1, D_MODEL)
    expand = (jnp.arange(LANES)[:, None] == (jnp.arange(D_MODEL)[None, :] // SSD_HEAD_DIM)).astype(BF16)
    cps = _tile(nc, 8)
    ns = nc // cps
    rows = cps * L
    row = lambda b, c: (b * ns + c, 0)
    return pl.pallas_call(
        functools.partial(_ssd_kernel, chunks_per_step=cps),
        grid=(batch, ns),
        in_specs=[
            pl.BlockSpec((rows, D_MODEL), row),
            pl.BlockSpec((SSD_CONV_DIM // LANES, rows, LANES), lambda b, c: (0, b * ns + c, 0)),
            pl.BlockSpec((rows, LANES), row),
            pl.BlockSpec((SSD_N_HEADS, rows), lambda b, c: (0, b * ns + c)),
            _const_spec((1, LANES)),
            _const_spec((SSD_N_HEADS, 1)),
            _const_spec((1, D_MODEL)),
            _const_spec((1, D_MODEL)),
            _const_spec((LANES, D_MODEL)),
        ],
        out_specs=pl.BlockSpec((rows, D_MODEL), row),
        out_shape=jax.ShapeDtypeStruct((t, D_MODEL), BF16),
        scratch_shapes=[pltpu.VMEM((D_MODEL // LANES, SSD_D_STATE, LANES), F32)],
        compiler_params=pltpu.CompilerParams(
            dimension_semantics=("arbitrary", "arbitrary"), vmem_limit_bytes=VMEM_LIMIT_BYTES),
        name="ssd_branch",
    )(zs, xc, dt, dtT, alog_row, alog_col, dexp, norm_w.reshape(1, -1), expand)


def _attn_kernel(qT_ref, k_ref, vT_ref, lq1_ref, lk1_ref, lq2_ref, lk2_ref, swc_ref, o_ref,
                 s_ref, p_ref, alpha_ref, m_ref, acc_ref, qm_ref, vext_ref, *, seq, tq):
    d = ATTN_HEAD_DIM
    dv = 2 * d
    tk = tq
    nq = seq // tq
    neg_inf = -jnp.inf

    key_i = lax.broadcasted_iota(jnp.int32, (tk, LANES), 0)
    qry_i = lax.broadcasted_iota(jnp.int32, (tk, LANES), 1)
    n_lt = tq // LANES
    feat = lax.broadcasted_iota(jnp.int32, (dv, LANES), 0)
    ones_rows = jnp.ones((ONES_ROWS, LANES), BF16)
    for ct in range(seq // LANES):
        q_c = qT_ref[:, ct * LANES:(ct + 1) * LANES]
        zero = jnp.zeros_like(q_c)
        qm_ref[0, ct] = jnp.where(feat < d, q_c, zero)
        qm_ref[1, ct] = jnp.where(feat >= d, q_c, zero)
        vext_ref[ct, 0:dv, :] = vT_ref[:, ct * LANES:(ct + 1) * LANES]
        vext_ref[ct, dv:, :] = ones_rows
    lam = (jnp.exp(jnp.sum(lq1_ref[...] * lk1_ref[...], axis=-1, keepdims=True))
           - jnp.exp(jnp.sum(lq2_ref[...] * lk2_ref[...], axis=-1, keepdims=True)) + LAMBDA_INIT)
    sw_col = swc_ref[...]

    def lane_tiles(ref, lead, first, count):
        return jnp.concatenate([ref[lead + (first + c,)] for c in range(count)], axis=1)

    steps = [(i, j) for i in range(1, nq) for j in range(i)] + [(i, i) for i in range(nq)]
    visited = set()
    first_visit = []
    for i, _ in steps:
        first_visit.append(i not in visited)
        visited.add(i)

    def stage_scores(t):
        i, j = steps[t]
        kb = k_ref[j * tk:(j + 1) * tk, :]
        for mp in range(2):
            s = jnp.dot(kb, lane_tiles(qm_ref, (mp,), i * n_lt, n_lt), preferred_element_type=F32)
            for c in range(n_lt):
                s_ref[t % 2, mp, c] = s[:, c * LANES:(c + 1) * LANES]

    def stage_softmax(t):
        i, j = steps[t]
        slot = t % 2
        for mp in range(2):
            for c in range(n_lt):

                def scores():
                    ss = s_ref[slot, mp, c]
                    if i == j:
                        ss = jnp.where(key_i <= qry_i + c * LANES, ss, neg_inf)
                    return ss

                m_new = jnp.max(scores(), axis=0, keepdims=True)
                if not first_visit[t]:
                    m_prev = m_ref[i, mp, c]
                    m_new = jnp.maximum(m_prev, m_new)
                    alpha_ref[slot, mp, c] = jnp.exp2(m_prev - m_new)
                p_ref[slot, mp, c] = jnp.exp2(scores() - m_new).astype(BF16)
                m_ref[i, mp, c] = m_new

    def stage_values(t):
        i, j = steps[t]
        slot = t % 2
        v_ext = lane_tiles(vext_ref, (), j * n_lt, n_lt)
        for mp in range(2):
            pv = jnp.dot(v_ext, lane_tiles(p_ref, (slot, mp), 0, n_lt), preferred_element_type=F32)
            for c in range(n_lt):
                pv_c = pv[:, c * LANES:(c + 1) * LANES]
                if first_visit[t]:
                    acc_ref[i, mp, c] = pv_c
                else:
                    acc_ref[i, mp, c] = alpha_ref[slot, mp, c] * acc_ref[i, mp, c] + pv_c

    n = len(steps)
    stage_scores(0)
    if n > 1:
        stage_scores(1)
    for t in range(n + 2):
        if t >= 2:
            stage_values(t - 2)
        if t < n:
            stage_softmax(t)
        if t + 2 < n:
            stage_scores(t + 2)

    for i in range(nq):
        for c in range(n_lt):
            inv_l = [1.0 / acc_ref[i, mp, c, dv:dv + 1, :] for mp in range(2)]
            oT = acc_ref[i, 0, c, 0:dv, :] * inv_l[0] - lam * (acc_ref[i, 1, c, 0:dv, :] * inv_l[1])
            ms = jnp.mean(oT * oT, axis=0, keepdims=True)
            y = oT * lax.rsqrt(ms + SUBLN_EPS) * sw_col * (1.0 - LAMBDA_INIT)
            o_ref[:, i * tq + c * LANES:i * tq + (c + 1) * LANES] = y.astype(BF16)


def _diff_attention(qT, k, vT, lq1, lk1, lq2, lk2, subln_w, batch, seq, tq):
    t = batch * seq
    vec = lambda a: a.reshape(1, -1)
    rows_blk = pl.BlockSpec((seq, LANES), lambda b, h: (b, h))
    cols_blk = pl.BlockSpec((LANES, seq), lambda b, h: (h, b))
    return pl.pallas_call(
        functools.partial(_attn_kernel, seq=seq, tq=tq),
        grid=(batch, ATTN_N_HEADS),
        in_specs=[
            cols_blk, rows_blk, cols_blk,
            _const_spec((1, ATTN_HEAD_DIM)),
            _const_spec((1, ATTN_HEAD_DIM)),
            _const_spec((1, ATTN_HEAD_DIM)),
            _const_spec((1, ATTN_HEAD_DIM)),
            _const_spec((2 * ATTN_HEAD_DIM, 1)),
        ],
        out_specs=cols_blk,
        out_shape=jax.ShapeDtypeStruct((D_MODEL, t), BF16),
        scratch_shapes=[
            pltpu.VMEM((2, 2, tq // LANES, tq, LANES), F32),
            pltpu.VMEM((2, 2, tq // LANES, tq, LANES), BF16),
            pltpu.VMEM((2, 2, tq // LANES, 1, LANES), F32),
            pltpu.VMEM((seq // tq, 2, tq // LANES, 1, LANES), F32),
            pltpu.VMEM((seq // tq, 2, tq // LANES, LANES + ONES_ROWS, LANES), F32),
            pltpu.VMEM((2, seq // LANES, LANES, LANES), BF16),
            pltpu.VMEM((seq // LANES, LANES + ONES_ROWS, LANES), BF16),
        ],
        compiler_params=pltpu.CompilerParams(
            dimension_semantics=("arbitrary", "arbitrary"),
            vmem_limit_bytes=VMEM_LIMIT_BYTES),
        name="diff_attention",
    )(qT, k, vT, vec(lq1), vec(lk1), vec(lq2), vec(lk2), subln_w.reshape(-1, 1))


def _merge_kernel(x_ref, ys_ref, ya_ref, gs_ref, ga_ref, wbs_ref, wba_ref, wo_ref, o_ref):
    a = jnp.dot(ys_ref[...], wbs_ref[...], preferred_element_type=F32)
    b = lax.dot_general(ya_ref[...], wba_ref[...], (((0,), (0,)), ((), ())), preferred_element_type=F32)
    merged = (jax.nn.sigmoid(gs_ref[...].astype(F32)) * a
              + jax.nn.sigmoid(ga_ref[...].astype(F32)) * b)
    o_ref[...] = x_ref[...] + jnp.dot(merged.astype(BF16), wo_ref[...], preferred_element_type=F32)


def _merge(x2d, ys, ya, gs, ga, wbs, wba, wo, tm):
    t, d = x2d.shape
    row = lambda i: (i, 0)
    tile = pl.BlockSpec((tm, d), row)
    return pl.pallas_call(
        _merge_kernel,
        grid=(t // tm,),
        in_specs=[tile, tile, pl.BlockSpec((d, tm), lambda i: (0, i)), tile, tile,
                  _const_spec((d, d)), _const_spec((d, d)), _const_spec((d, d))],
        out_specs=tile,
        out_shape=jax.ShapeDtypeStruct((t, d), F32),
        compiler_params=pltpu.CompilerParams(
            dimension_semantics=("arbitrary",), vmem_limit_bytes=VMEM_LIMIT_BYTES),
        name="gated_merge_out_projection",
    )(x2d, ys, ya, gs, ga, wbs, wba, wo)


def _ffn_kernel(x_ref, halo_ref, nw_ref, wup_ref, cw_ref, cb_ref, wd_ref, fw_ref, o_ref, gbuf_ref, act_ref,
                *, tm, tiles_per_seq, chunk):
    i = pl.program_id(0)
    x1 = x_ref[...]
    x_ext = jnp.concatenate([halo_ref[...], x1], axis=0)
    h_ext = _rms(x_ext, nw_ref[...], NORM_EPS).astype(BF16)
    h2 = h_ext[FFN_HALO:, :]
    keep_halo = jnp.where(i % tiles_per_seq == 0, 0.0, 1.0)
    cw = cw_ref[...]
    cb = cb_ref[...]
    KW = FFN_CONV_WIDTH
    n_chunks = D_FF // chunk

    def up(c):
        lo = c * chunk
        gate_ext = jnp.dot(h_ext, wup_ref[:, lo:lo + chunk], preferred_element_type=F32)
        val = jnp.dot(h2, wup_ref[:, D_FF + lo:D_FF + lo + chunk], preferred_element_type=F32)
        return gate_ext, val

    acc = jnp.zeros((tm, D_MODEL), F32)
    nxt = up(0)
    for c in range(n_chunks):
        lo = c * chunk
        gate_ext, val = nxt
        if c + 1 < n_chunks:
            nxt = up(c + 1)
        slot = c % 2
        gbuf_ref[slot, 0:FFN_HALO, :] = gate_ext[0:FFN_HALO, :] * keep_halo
        gbuf_ref[slot, FFN_HALO:, :] = gate_ext[FFN_HALO:, :]
        conv = cb[:, lo:lo + chunk] + cw[KW - 1:KW, lo:lo + chunk] * gate_ext[FFN_HALO:, :]
        for j in range(KW - 1):
            shift = KW - 1 - j
            conv = conv + cw[j:j + 1, lo:lo + chunk] * gbuf_ref[slot, pl.ds(FFN_HALO - shift, tm), :]
        act_ref[:, lo:lo + chunk] = (_silu(conv) * val).astype(BF16)
        if (c + 1) % DOWN_GROUP == 0 or c + 1 == n_chunks:
            k0 = (c // DOWN_GROUP) * DOWN_GROUP * chunk
            acc = acc + jnp.dot(act_ref[:, k0:lo + chunk], wd_ref[k0:lo + chunk, :], preferred_element_type=F32)
    o_ref[...] = _rms(x1 + acc, fw_ref[...], NORM_EPS)


def _conv_ffn(x1, norm_w, w_up, conv_w, conv_b, w_down, final_w, seq, tm, chunk):
    t, d = x1.shape
    tiles_per_seq = seq // tm
    halo_blocks = tm // FFN_HALO
    return pl.pallas_call(
        functools.partial(_ffn_kernel, tm=tm, tiles_per_seq=tiles_per_seq, chunk=chunk),
        grid=(t // tm,),
        in_specs=[
            pl.BlockSpec((tm, d), lambda i: (i, 0)),
            pl.BlockSpec((FFN_HALO, d), lambda i: (jnp.maximum(i * halo_blocks - 1, 0), 0)),
            _const_spec((1, d)),
            _const_spec(w_up.shape),
            _const_spec(conv_w.shape),
            _const_spec((1, D_FF)),
            _const_spec(w_down.shape),
            _const_spec((1, d)),
        ],
        out_specs=pl.BlockSpec((tm, d), lambda i: (i, 0)),
        out_shape=jax.ShapeDtypeStruct((t, d), F32),
        scratch_shapes=[pltpu.VMEM((2, FFN_HALO + tm, chunk), F32), pltpu.VMEM((tm, D_FF), BF16)],
        compiler_params=pltpu.CompilerParams(
            dimension_semantics=("arbitrary",), vmem_limit_bytes=VMEM_LIMIT_BYTES),
        name="conv_ffn_final_norm",
    )(x1, x1, norm_w.reshape(1, -1), w_up, conv_w, conv_b.reshape(1, -1), w_down, final_w.reshape(1, -1))


def _rope_tables(seq):
    half = ATTN_HEAD_DIM // 2
    inv = 1.0 / (ROPE_THETA ** (jnp.arange(0, ATTN_HEAD_DIM, 2, dtype=F32) / ATTN_HEAD_DIM))
    ang = jnp.arange(seq, dtype=F32)[:, None] * inv[None, :]
    lane = jnp.arange(LANES)
    cos = jnp.cos(ang)[:, lane % half]
    sin = jnp.sin(ang)[:, lane % half]
    second = (lane % ATTN_HEAD_DIM) >= half
    sa = jnp.where(second[None, :], sin, 0.0)
    sb = jnp.where(second[None, :], 0.0, -sin)
    return cos, sa, sb, jnp.cos(ang).T, jnp.sin(ang).T


def _tile(n, pref):
    while n % pref:
        pref //= 2
    return pref


def kernel(x, norm_mix_w, w_in, ssd_conv_w, ssd_conv_b, ssd_dt_bias, ssd_a_log, ssd_d_skip, ssd_norm_w, lambda_q1, lambda_k1, lambda_q2, lambda_k2, subln_w, w_branch_ssd, w_branch_attn, w_out, norm_ffn_w, w_up, ffn_conv_w, ffn_conv_b, w_down, final_norm_w):
    batch, seq, d = x.shape
    assert d == D_MODEL and seq % SSD_CHUNK == 0
    t = batch * seq
    x2d = x.reshape(t, d)
    li = 0

    w = w_in[li]
    o_z, o_xbc = 0, D_MODEL
    o_dt = o_xbc + SSD_CONV_DIM
    o_q = o_dt + SSD_N_HEADS
    cols = lambda lo, n: w[:, lo:lo + n]
    w_dt = cols(o_dt, SSD_N_HEADS)
    w_all = jnp.concatenate([
        cols(o_z, D_MODEL), cols(o_q + D_MODEL, D_MODEL),
        cols(o_q + 3 * D_MODEL, D_MODEL), cols(o_q + 4 * D_MODEL, D_MODEL), cols(o_xbc, SSD_CONV_DIM),
        jnp.pad(w_dt, ((0, 0), (0, LANES - SSD_N_HEADS)))], axis=1).astype(BF16)
    assert w_all.shape[1] == _IN_COLS_PADDED
    w_dtT = w_dt.T.astype(BF16)
    w_qT = cols(o_q, D_MODEL).T.astype(BF16)
    w_vT = cols(o_q + 2 * D_MODEL, D_MODEL).T.astype(BF16)

    tm = _tile(seq, 512)
    zs, qT, k, vT, gs, ga, xc, dt, dtT = _in_projection(
        x2d, norm_mix_w[li].reshape(1, -1), w_all, w_qT, w_vT, w_dtT, ssd_conv_w[li], ssd_conv_b[li],
        ssd_dt_bias[li], _rope_tables(seq), seq, tm)

    y_ssd = _ssd_branch(zs, xc, dt, dtT, ssd_a_log[li], ssd_d_skip[li], ssd_norm_w[li], batch, seq)
    tq = _tile(seq, 256)
    y_attn = _diff_attention(qT, k, vT, lambda_q1[li], lambda_k1[li], lambda_q2[li], lambda_k2[li],
                             subln_w[li], batch, seq, tq)
    x1 = _merge(x2d, y_ssd, y_attn, gs, ga, w_branch_ssd[li].astype(BF16), w_branch_attn[li].astype(BF16),
                w_out[li].astype(BF16), _tile(seq, 1024))
    out = _conv_ffn(x1, norm_ffn_w[li], w_up[li].astype(BF16), ffn_conv_w[li], ffn_conv_b[li],
                    w_down[li].astype(BF16), final_norm_w, seq, _tile(seq, 1024), 256)
    return out.reshape(batch, seq, d)
```
